```python
import jax, jax.numpy as jnp
from jax import lax
import numpy as np

D_MODEL = 1024
BATCH = 2
SEQ = 8192
DEPTH = 1
DEC_BATCH = 8
DEC_SEQ = 8192
PAST_LEN = 128

GRID_W = 64
N_HEADS = 8
N_KV_HEADS = 2
HEAD_DIM = 64
GQA_GROUP = N_HEADS // N_KV_HEADS
ATTN_W = N_HEADS * HEAD_DIM
KV_W = N_KV_HEADS * HEAD_DIM
Q_BLOCK = 128
ROPE_THETA = 10000.0
ROPE_AXIS_DIM = HEAD_DIM // 2
ROPE_PAIRS = ROPE_AXIS_DIM // 2
CONV_CH = 512
CONV_WIDTH = 31
CONV_PAD = CONV_WIDTH // 2
N_MEM = 256
MEM_HEADS = 4
MEM_HEAD_DIM = 128
MEM_W = MEM_HEADS * MEM_HEAD_DIM
N_BRANCH = 3
D_FF = 4 * D_MODEL
RMS_EPS = 1e-6
LN_EPS = 1e-5
OFF_Q = 0
OFF_K = OFF_Q + ATTN_W
OFF_V = OFF_K + KV_W
OFF_GLU = OFF_V + KV_W
OFF_XQ = OFF_GLU + 2 * CONV_CH
OFF_GATE = OFF_XQ + MEM_W
IN_COLS = OFF_GATE + N_BRANCH * D_MODEL

kernel_name = "hybrid_gqa_conformer_memory_encoder"


def rms_norm(x, g):
    xf = x.astype(jnp.float32)
    y = xf * lax.rsqrt(jnp.mean(xf * xf, axis=-1, keepdims=True) + RMS_EPS)
    return (y * g.astype(jnp.float32)).astype(x.dtype)


def layer_norm(x, g, b):
    xf = x.astype(jnp.float32)
    mu = jnp.mean(xf, axis=-1, keepdims=True)
    xc = xf - mu
    var = jnp.mean(xc * xc, axis=-1, keepdims=True)
    y = xc * lax.rsqrt(var + LN_EPS) * g.astype(jnp.float32) + b.astype(jnp.float32)
    return y.astype(x.dtype)


def axial_rope_tables(T):
    rows = T // GRID_W
    row_idx = jnp.repeat(jnp.arange(rows), GRID_W)
    col_idx = jnp.tile(jnp.arange(GRID_W), rows)
    inv_freq = ROPE_THETA ** (-jnp.arange(ROPE_PAIRS, dtype=jnp.float32) / ROPE_PAIRS)
    ang_r = row_idx.astype(jnp.float32)[:, None] * inv_freq[None, :]
    ang_c = col_idx.astype(jnp.float32)[:, None] * inv_freq[None, :]
    return jnp.cos(ang_r), jnp.sin(ang_r), jnp.cos(ang_c), jnp.sin(ang_c)


def _rotate(x, cos, sin):
    x1, x2 = x[..., :ROPE_PAIRS], x[..., ROPE_PAIRS:]
    c = cos[:, None, :]
    s = sin[:, None, :]
    return jnp.concatenate([x1 * c - x2 * s, x2 * c + x1 * s], axis=-1)


def apply_axial_rope(x, tabs):
    cos_r, sin_r, cos_c, sin_c = tabs
    xf = x.astype(jnp.float32)
    out = jnp.concatenate([_rotate(xf[..., :ROPE_AXIS_DIM], cos_r, sin_r),
                           _rotate(xf[..., ROPE_AXIS_DIM:], cos_c, sin_c)], axis=-1)
    return out.astype(x.dtype)


def blocked_gqa(q, k, v):
    B, T = q.shape[0], q.shape[1]
    nb = T // Q_BLOCK
    qb = q.reshape(B, nb, Q_BLOCK, N_KV_HEADS, GQA_GROUP, HEAD_DIM).transpose(1, 0, 2, 3, 4, 5)
    scale = HEAD_DIM ** -0.5

    def one_block(q_blk):
        s = jnp.einsum('bqkgd,bskd->bkgqs', q_blk, k).astype(jnp.float32) * scale
        p = jax.nn.softmax(s, axis=-1).astype(v.dtype)
        return jnp.einsum('bkgqs,bskd->bqkgd', p, v)

    o = lax.map(one_block, qb)
    return o.transpose(1, 0, 2, 3, 4, 5).reshape(B, T, ATTN_W)


def depthwise_conv(u, w, b):
    y = lax.conv_general_dilated(u, w[:, None, :].astype(u.dtype), window_strides=(1,),
                                 padding=[(CONV_PAD, CONV_PAD)],
                                 dimension_numbers=('NWC', 'WIO', 'NWC'),
                                 feature_group_count=CONV_CH)
    return y + b.astype(u.dtype)


def encoder_layer(x, mem, tabs, attn_norm, w_in, q_norm, k_norm, w_attn_o,
                  conv_w, conv_b, conv_ln_g, conv_ln_b, w_conv_o,
                  mem_norm, w_mem_kv, w_mem_o, w_out, mlp_norm, w_up, w_down):
    B, T, _ = x.shape
    xn = rms_norm(x, attn_norm)
    z = xn @ w_in

    q = z[..., OFF_Q:OFF_K].reshape(B, T, N_HEADS, HEAD_DIM)
    k = z[..., OFF_K:OFF_V].reshape(B, T, N_KV_HEADS, HEAD_DIM)
    v = z[..., OFF_V:OFF_GLU].reshape(B, T, N_KV_HEADS, HEAD_DIM)
    q = apply_axial_rope(rms_norm(q, q_norm), tabs)
    k = apply_axial_rope(rms_norm(k, k_norm), tabs)
    attn_out = blocked_gqa(q, k, v) @ w_attn_o

    glu = z[..., OFF_GLU:OFF_XQ]
    u = glu[..., :CONV_CH] * jax.nn.sigmoid(glu[..., CONV_CH:])
    c = depthwise_conv(u, conv_w, conv_b)
    c = jax.nn.silu(layer_norm(c, conv_ln_g, conv_ln_b))
    conv_out = c @ w_conv_o

    mn = rms_norm(mem, mem_norm)
    kv_m = mn @ w_mem_kv
    k_m = kv_m[..., :MEM_W].reshape(B, N_MEM, MEM_HEADS, MEM_HEAD_DIM)
    v_m = kv_m[..., MEM_W:].reshape(B, N_MEM, MEM_HEADS, MEM_HEAD_DIM)
    xq = z[..., OFF_XQ:OFF_GATE].reshape(B, T, MEM_HEADS, MEM_HEAD_DIM)
    s_m = jnp.einsum('bthd,bmhd->bhtm', xq, k_m).astype(jnp.float32) * (MEM_HEAD_DIM ** -0.5)
    p_m = jax.nn.softmax(s_m, axis=-1).astype(v_m.dtype)
    mem_out = jnp.einsum('bhtm,bmhd->bthd', p_m, v_m).reshape(B, T, MEM_W) @ w_mem_o

    g = jax.nn.sigmoid(z[..., OFF_GATE:]).reshape(B, T, N_BRANCH, D_MODEL)
    merged = g[:, :, 0] * attn_out + g[:, :, 1] * conv_out + g[:, :, 2] * mem_out
    h = x + merged @ w_out

    hn = rms_norm(h, mlp_norm)
    a = jnp.square(jax.nn.relu(hn @ w_up))
    return h + a @ w_down


def trunk(x, mem, attn_norm, w_in, q_norm, k_norm, w_attn_o, conv_w, conv_b, conv_ln_g,
          conv_ln_b, w_conv_o, mem_norm, w_mem_kv, w_mem_o, w_out, mlp_norm, w_up, w_down,
          final_norm):
    tabs = axial_rope_tables(x.shape[1])
    for l in range(DEPTH):
        x = encoder_layer(x, mem, tabs, attn_norm[l], w_in[l], q_norm[l], k_norm[l], w_attn_o[l],
                          conv_w[l], conv_b[l], conv_ln_g[l], conv_ln_b[l], w_conv_o[l],
                          mem_norm[l], w_mem_kv[l], w_mem_o[l], w_out[l], mlp_norm[l],
                          w_up[l], w_down[l])
    return rms_norm(x, final_norm)


def setup_inputs(seed: int = 0) -> dict:
    key = jax.random.key(seed)
    ks = jax.random.split(key, 24)
    f32 = jnp.float32

    def nrm(k, shape, scale):
        return jax.random.normal(k, shape, f32) * scale

    def gain(k, shape):
        return 1.0 + 0.02 * jax.random.normal(k, shape, f32)

    L = DEPTH
    return {
        "x_prompt": nrm(ks[0], (BATCH, SEQ, D_MODEL), 1.0),
        "x_sample": nrm(ks[1], (DEC_BATCH, DEC_SEQ, D_MODEL), 1.0),
        "mem_prompt": nrm(ks[2], (BATCH, N_MEM, D_MODEL), 1.0),
        "mem_sample": nrm(ks[3], (DEC_BATCH, N_MEM, D_MODEL), 1.0),
        "attn_norm": gain(ks[4], (L, D_MODEL)),
        "w_in": nrm(ks[5], (L, D_MODEL, IN_COLS), D_MODEL ** -0.5),
        "q_norm": gain(ks[6], (L, HEAD_DIM)),
        "k_norm": gain(ks[7], (L, HEAD_DIM)),
        "w_attn_o": nrm(ks[8], (L, ATTN_W, D_MODEL), ATTN_W ** -0.5),
        "conv_w": nrm(ks[9], (L, CONV_WIDTH, CONV_CH), CONV_WIDTH ** -0.5),
        "conv_b": nrm(ks[10], (L, CONV_CH), 0.02),
        "conv_ln_g": gain(ks[11], (L, CONV_CH)),
        "conv_ln_b": nrm(ks[12], (L, CONV_CH), 0.02),
        "w_conv_o": nrm(ks[13], (L, CONV_CH, D_MODEL), CONV_CH ** -0.5),
        "mem_norm": gain(ks[14], (L, D_MODEL)),
        "w_mem_kv": nrm(ks[15], (L, D_MODEL, 2 * MEM_W), D_MODEL ** -0.5),
        "w_mem_o": nrm(ks[16], (L, MEM_W, D_MODEL), MEM_W ** -0.5),
        "w_out": nrm(ks[17], (L, D_MODEL, D_MODEL), D_MODEL ** -0.5),
        "mlp_norm": gain(ks[18], (L, D_MODEL)),
        "w_up": nrm(ks[19], (L, D_MODEL, D_FF), D_MODEL ** -0.5),
        "w_down": nrm(ks[20], (L, D_FF, D_MODEL), D_FF ** -0.5),
        "final_norm": gain(ks[21], (D_MODEL,)),
    }


def reference(x_prompt, x_sample, mem_prompt, mem_sample, attn_norm, w_in, q_norm, k_norm,
              w_attn_o, conv_w, conv_b, conv_ln_g, conv_ln_b, w_conv_o, mem_norm, w_mem_kv,
              w_mem_o, w_out, mlp_norm, w_up, w_down, final_norm):
    y_prompt = trunk(x_prompt, mem_prompt, attn_norm, w_in, q_norm, k_norm, w_attn_o, conv_w,
                     conv_b, conv_ln_g, conv_ln_b, w_conv_o, mem_norm, w_mem_kv, w_mem_o, w_out,
                     mlp_norm, w_up, w_down, final_norm)
    y_sample = trunk(x_sample, mem_sample, attn_norm, w_in, q_norm, k_norm, w_attn_o, conv_w,
                     conv_b, conv_ln_g, conv_ln_b, w_conv_o, mem_norm, w_mem_kv, w_mem_o, w_out,
                     mlp_norm, w_up, w_down, final_norm)
    return (y_prompt, y_sample)
```

```python
import functools

import jax
import jax.numpy as jnp
from jax import lax
from jax.experimental import pallas as pl
from jax.experimental.pallas import tpu as pltpu

F32 = jnp.float32
BF16 = jnp.bfloat16

D_MODEL = 1024
GRID_W = 64
N_HEADS = 8
N_KV_HEADS = 2
HEAD_DIM = 64
GQA_GROUP = N_HEADS // N_KV_HEADS
ATTN_W = N_HEADS * HEAD_DIM
KV_W = N_KV_HEADS * HEAD_DIM
GROUP_W = GQA_GROUP * HEAD_DIM
ROPE_THETA = 10000.0
ROPE_PAIRS = HEAD_DIM // 4
CONV_CH = 512
CONV_WIDTH = 31
CONV_PAD = CONV_WIDTH // 2
N_MEM = 256
MEM_HEADS = 4
MEM_HEAD_DIM = 128
MEM_W = MEM_HEADS * MEM_HEAD_DIM
N_BRANCH = 3
D_FF = 4 * D_MODEL
RMS_EPS = 1e-6
LN_EPS = 1e-5
OFF_Q = 0
OFF_K = OFF_Q + ATTN_W
OFF_V = OFF_K + KV_W
OFF_GLU = OFF_V + KV_W
OFF_XQ = OFF_GLU + 2 * CONV_CH
OFF_GATE = OFF_XQ + MEM_W
IN_COLS = OFF_GATE + N_BRANCH * D_MODEL

LANES = 128
CONV_HALO = 16
NEG_BIG = -1e30
VMEM_LIMIT_BYTES = 56 * 1024 * 1024


def _params(n_grid_dims):
    return pltpu.CompilerParams(
        dimension_semantics=("arbitrary",) * n_grid_dims,
        vmem_limit_bytes=VMEM_LIMIT_BYTES)


def _rms_norm(x, gain):
    ms = jnp.mean(x * x, axis=-1, keepdims=True)
    return x * lax.rsqrt(ms + RMS_EPS) * gain


def _mem_kv_kernel(mem_ref, g_ref, w_ref, kt_ref, va_ref):
    mn = _rms_norm(mem_ref[0], g_ref[...]).astype(BF16)
    kv = jnp.dot(mn, w_ref[...], preferred_element_type=F32)
    kt_ref[0] = kv[:, :MEM_W].T.astype(BF16)
    ones = jnp.ones((N_MEM, MEM_HEAD_DIM), BF16)
    for h in range(MEM_HEADS):
        lo = MEM_W + h * MEM_HEAD_DIM
        va_ref[0, h, :, :MEM_HEAD_DIM] = kv[:, lo:lo + MEM_HEAD_DIM].astype(BF16)
        va_ref[0, h, :, MEM_HEAD_DIM:] = ones


def _mem_kv(mem, g, w):
    B = mem.shape[0]
    return pl.pallas_call(
        _mem_kv_kernel,
        grid=(B,),
        in_specs=[
            pl.BlockSpec((1, N_MEM, D_MODEL), lambda b: (b, 0, 0)),
            pl.BlockSpec((1, D_MODEL), lambda b: (0, 0)),
            pl.BlockSpec((D_MODEL, 2 * MEM_W), lambda b: (0, 0)),
        ],
        out_specs=[
            pl.BlockSpec((1, MEM_W, N_MEM), lambda b: (b, 0, 0)),
            pl.BlockSpec((1, MEM_HEADS, N_MEM, 2 * MEM_HEAD_DIM), lambda b: (b, 0, 0, 0)),
        ],
        out_shape=[
            jax.ShapeDtypeStruct((B, MEM_W, N_MEM), BF16),
            jax.ShapeDtypeStruct((B, MEM_HEADS, N_MEM, 2 * MEM_HEAD_DIM), BF16),
        ],
        compiler_params=_params(1),
        name="mem_kv",
    )(mem, g, w)


def _head_norm_rope(z, gain, bd, cos, sin_up, sin_dn, scale):
    zz = z * z
    hi = zz.astype(BF16)
    lo = (zz - hi.astype(F32)).astype(BF16)
    ssq = (jnp.dot(hi, bd, preferred_element_type=F32)
           + jnp.dot(lo, bd, preferred_element_type=F32))
    y = z * lax.rsqrt(ssq * (1.0 / HEAD_DIM) + RMS_EPS) * gain
    up = pltpu.roll(y, LANES - ROPE_PAIRS, 1)
    dn = pltpu.roll(y, ROPE_PAIRS, 1)
    return (y * cos + up * sin_up + dn * sin_dn) * scale


def _in_proj_kernel(x_ref, g_ref, w_ref, gq_ref, gk_ref, bd_ref, cos_ref, sup_ref, sdn_ref,
                    q_ref, k_ref, v_ref, u_ref):
    xn = _rms_norm(x_ref[0], g_ref[...]).astype(BF16)
    z = jnp.dot(xn, w_ref[...], preferred_element_type=F32)
    bd = bd_ref[...]
    cos, sup, sdn = cos_ref[...], sup_ref[...], sdn_ref[...]
    q_scale = HEAD_DIM ** -0.5
    for c in range(ATTN_W // LANES):
        zc = z[:, OFF_Q + c * LANES:OFF_Q + (c + 1) * LANES]
        q_ref[0, :, c * LANES:(c + 1) * LANES] = _head_norm_rope(
            zc, gq_ref[...], bd, cos, sup, sdn, q_scale).astype(BF16)
    k_ref[0] = _head_norm_rope(z[:, OFF_K:OFF_V], gk_ref[...], bd, cos, sup, sdn, 1.0).astype(BF16)
    v_ref[0] = z[:, OFF_V:OFF_GLU].astype(BF16)
    a = z[:, OFF_GLU:OFF_GLU + CONV_CH]
    b = z[:, OFF_GLU + CONV_CH:OFF_XQ]
    u_ref[0] = (a * jax.nn.sigmoid(b)).astype(BF16)


def _in_proj(x, g, w, gq, gk, bd, cos, sup, sdn, tm):
    B, T, _ = x.shape
    const = lambda b, i: (0, 0)
    tab = pl.BlockSpec((tm, LANES), lambda b, i: (i, 0))
    return pl.pallas_call(
        _in_proj_kernel,
        grid=(B, T // tm),
        in_specs=[
            pl.BlockSpec((1, tm, D_MODEL), lambda b, i: (b, i, 0)),
            pl.BlockSpec((1, D_MODEL), const),
            pl.BlockSpec((D_MODEL, OFF_XQ), const),
            pl.BlockSpec((1, LANES), const),
            pl.BlockSpec((1, LANES), const),
            pl.BlockSpec((LANES, LANES), const),
            tab, tab, tab,
        ],
        out_specs=[
            pl.BlockSpec((1, tm, ATTN_W), lambda b, i: (b, i, 0)),
            pl.BlockSpec((1, tm, KV_W), lambda b, i: (b, i, 0)),
            pl.BlockSpec((1, tm, KV_W), lambda b, i: (b, i, 0)),
            pl.BlockSpec((1, tm, CONV_CH), lambda b, i: (b, i, 0)),
        ],
        out_shape=[
            jax.ShapeDtypeStruct((B, T, ATTN_W), BF16),
            jax.ShapeDtypeStruct((B, T, KV_W), BF16),
            jax.ShapeDtypeStruct((B, T, KV_W), BF16),
            jax.ShapeDtypeStruct((B, T, CONV_CH), BF16),
        ],
        compiler_params=_params(2),
        name="in_proj",
    )(x, g, w, gq, gk, bd, cos, sup, sdn)


def _flash_kernel(q_ref, k_ref, v_ref, o_ref, vt_ref, qp_ref, m_ref, acc_ref, *, tk, tq, n_kv):
    g = pl.program_id(1)
    j = pl.program_id(2)
    is_g0 = g == 0

    @pl.when(j == 0)
    def _():
        ones = jnp.ones((HEAD_DIM, tk), BF16)

        def build(c, carry):
            vc = v_ref[0, pl.ds(pl.multiple_of(c * tk, tk), tk), :].astype(F32)
            vct = vc.T
            top = jnp.where(is_g0, vct[:HEAD_DIM], vct[HEAD_DIM:])
            vt_ref[c, :HEAD_DIM, :] = top.astype(BF16)
            vt_ref[c, HEAD_DIM:, :] = ones
            return carry

        lax.fori_loop(0, n_kv, build, 0)

    qt = q_ref[0].astype(F32).T
    zeros = jnp.zeros((HEAD_DIM, tq), F32)
    for i in range(GQA_GROUP):
        qi = qt[i * HEAD_DIM:(i + 1) * HEAD_DIM]
        qp_ref[i, :HEAD_DIM, :] = jnp.where(is_g0, qi, zeros).astype(BF16)
        qp_ref[i, HEAD_DIM:, :] = jnp.where(is_g0, zeros, qi).astype(BF16)
    m_ref[...] = jnp.full(m_ref.shape, NEG_BIG, F32)
    acc_ref[...] = jnp.zeros(acc_ref.shape, F32)

    def kv_step(s_idx, carry):
        kb = k_ref[0, pl.ds(pl.multiple_of(s_idx * tk, tk), tk), :]
        vtb = vt_ref[s_idx]
        for i in range(GQA_GROUP):
            s = jnp.dot(kb, qp_ref[i], preferred_element_type=F32)
            m_old = m_ref[i]
            m_new = jnp.maximum(m_old, jnp.max(s, axis=0, keepdims=True))
            p = jnp.exp(s - m_new).astype(BF16)
            alpha = jnp.exp(m_old - m_new)
            pv = jnp.dot(vtb, p, preferred_element_type=F32)
            acc_ref[i] = acc_ref[i] * alpha + pv
            m_ref[i] = m_new
        return carry

    lax.fori_loop(0, n_kv, kv_step, 0)

    outs = []
    for i in range(GQA_GROUP):
        a = acc_ref[i]
        outs.append(a[:HEAD_DIM] / a[HEAD_DIM:HEAD_DIM + 1])
    o_ref[0] = jnp.concatenate(outs, axis=0).T.astype(BF16)


def _flash(q, k, v, tq, tk):
    B, T, _ = q.shape
    n_kv = T // tk
    kern = functools.partial(_flash_kernel, tk=tk, tq=tq, n_kv=n_kv)
    return pl.pallas_call(
        kern,
        grid=(B, N_KV_HEADS, T // tq),
        in_specs=[
            pl.BlockSpec((1, tq, GROUP_W), lambda b, g, j: (b, j, g)),
            pl.BlockSpec((1, T, KV_W), lambda b, g, j: (b, 0, 0)),
            pl.BlockSpec((1, T, KV_W), lambda b, g, j: (b, 0, 0)),
        ],
        out_specs=pl.BlockSpec((1, tq, GROUP_W), lambda b, g, j: (b, j, g)),
        out_shape=jax.ShapeDtypeStruct((B, T, ATTN_W), BF16),
        scratch_shapes=[
            pltpu.VMEM((n_kv, 2 * HEAD_DIM, tk), BF16),
            pltpu.VMEM((GQA_GROUP, 2 * HEAD_DIM, tq), BF16),
            pltpu.VMEM((GQA_GROUP, 1, tq), F32),
            pltpu.VMEM((GQA_GROUP, 2 * HEAD_DIM, tq), F32),
        ],
        compiler_params=_params(3),
        name="flash",
    )(q, k, v)


def _conv_kernel(prev_ref, cur_ref, next_ref, w_ref, b_ref, lg_ref, lb_ref, o_ref, ext_ref,
                 *, tm, rows):
    i = pl.program_id(1)
    last = pl.num_programs(1) - 1
    halo = jnp.zeros((CONV_HALO, CONV_CH), F32)
    ext_ref[:CONV_HALO, :] = jnp.where(i > 0, prev_ref[0].astype(F32), halo)
    ext_ref[CONV_HALO:CONV_HALO + tm, :] = cur_ref[0].astype(F32)
    ext_ref[CONV_HALO + tm:, :] = jnp.where(i < last, next_ref[0].astype(F32), halo)
    win_rows = rows + 2 * CONV_HALO
    first = CONV_HALO - CONV_PAD

    def chunk(c, carry):
        r0 = pl.multiple_of(c * rows, rows)
        win = ext_ref[pl.ds(r0, win_rows), :]
        acc = jnp.zeros((rows, CONV_CH), F32) + b_ref[...]
        for r in range(8):
            wr = win if r == 0 else pltpu.roll(win, win_rows - r, 0)
            for a in range((2 * CONV_HALO) // 8):
                tap = 8 * a + r - first
                if 0 <= tap < CONV_WIDTH:
                    acc = acc + wr[8 * a:8 * a + rows] * w_ref[tap:tap + 1, :]
        mu = jnp.mean(acc, axis=-1, keepdims=True)
        xc = acc - mu
        var = jnp.mean(xc * xc, axis=-1, keepdims=True)
        y = xc * lax.rsqrt(var + LN_EPS) * lg_ref[...] + lb_ref[...]
        o_ref[0, pl.ds(r0, rows), :] = (y * jax.nn.sigmoid(y)).astype(BF16)
        return carry

    lax.fori_loop(0, tm // rows, chunk, 0)


def _conv(u, w, b, lg, lb, tm, rows):
    B, T, _ = u.shape
    hb = tm // CONV_HALO
    n_halo_blocks = T // CONV_HALO
    kern = functools.partial(_conv_kernel, tm=tm, rows=rows)
    const = lambda b_, i: (0, 0)
    return pl.pallas_call(
        kern,
        grid=(B, T // tm),
        in_specs=[
            pl.BlockSpec((1, CONV_HALO, CONV_CH),
                         lambda b_, i: (b_, jnp.maximum(i * hb - 1, 0), 0)),
            pl.BlockSpec((1, tm, CONV_CH), lambda b_, i: (b_, i, 0)),
            pl.BlockSpec((1, CONV_HALO, CONV_CH),
                         lambda b_, i: (b_, jnp.minimum((i + 1) * hb, n_halo_blocks - 1), 0)),
            pl.BlockSpec((CONV_WIDTH, CONV_CH), const),
            pl.BlockSpec((1, CONV_CH), const),
            pl.BlockSpec((1, CONV_CH), const),
            pl.BlockSpec((1, CONV_CH), const),
        ],
        out_specs=pl.BlockSpec((1, tm, CONV_CH), lambda b_, i: (b_, i, 0)),
        out_shape=jax.ShapeDtypeStruct((B, T, CONV_CH), BF16),
        scratch_shapes=[pltpu.VMEM((tm + 2 * CONV_HALO, CONV_CH), F32)],
        compiler_params=_params(2),
        name="conv",
    )(u, u, u, w, b, lg, lb)


def _merge_kernel(x_ref, attn_ref, c_ref, kt_ref, va_ref, g_ref, wb_ref, wa_ref, wc_ref, wm_ref,
                  wo_ref, h_ref):
    x = x_ref[0]
    xn = _rms_norm(x, g_ref[...]).astype(BF16)
    zb = jnp.dot(xn, wb_ref[...], preferred_element_type=F32)
    xq = zb[:, :MEM_W].astype(BF16)
    mem_scale = MEM_HEAD_DIM ** -0.5
    heads = []
    for h in range(MEM_HEADS):
        lo = h * MEM_HEAD_DIM
        s = jnp.dot(xq[:, lo:lo + MEM_HEAD_DIM], kt_ref[0, lo:lo + MEM_HEAD_DIM, :],
                    preferred_element_type=F32) * mem_scale
        p = jnp.exp(s - jnp.max(s, axis=-1, keepdims=True)).astype(BF16)
        oa = jnp.dot(p, va_ref[0, h], preferred_element_type=F32)
        heads.append(oa[:, :MEM_HEAD_DIM] / oa[:, MEM_HEAD_DIM:])
    mo = jnp.concatenate(heads, axis=-1).astype(BF16)
    attn_out = jnp.dot(attn_ref[0], wa_ref[...], preferred_element_type=F32)
    conv_out = jnp.dot(c_ref[0], wc_ref[...], preferred_element_type=F32)
    mem_out = jnp.dot(mo, wm_ref[...], preferred_element_type=F32)
    g0 = jax.nn.sigmoid(zb[:, MEM_W:MEM_W + D_MODEL])
    g1 = jax.nn.sigmoid(zb[:, MEM_W + D_MODEL:MEM_W + 2 * D_MODEL])
    g2 = jax.nn.sigmoid(zb[:, MEM_W + 2 * D_MODEL:])
    merged = (g0 * attn_out + g1 * conv_out + g2 * mem_out).astype(BF16)
    h_ref[0] = x + jnp.dot(merged, wo_ref[...], preferred_element_type=F32)


def _merge(x, attn, c, kt, va, g, wb, wa, wc, wm, wo, tm):
    B, T, _ = x.shape
    const = lambda b, i: (0, 0)
    tok = lambda w: pl.BlockSpec((1, tm, w), lambda b, i: (b, i, 0))
    return pl.pallas_call(
        _merge_kernel,
        grid=(B, T // tm),
        in_specs=[
            tok(D_MODEL), tok(ATTN_W), tok(CONV_CH),
            pl.BlockSpec((1, MEM_W, N_MEM), lambda b, i: (b, 0, 0)),
            pl.BlockSpec((1, MEM_HEADS, N_MEM, 2 * MEM_HEAD_DIM), lambda b, i: (b, 0, 0, 0)),
            pl.BlockSpec((1, D_MODEL), const),
            pl.BlockSpec((D_MODEL, MEM_W + N_BRANCH * D_MODEL), const),
            pl.BlockSpec((ATTN_W, D_MODEL), const),
            pl.BlockSpec((CONV_CH, D_MODEL), const),
            pl.BlockSpec((MEM_W, D_MODEL), const),
            pl.BlockSpec((D_MODEL, D_MODEL), const),
        ],
        out_specs=tok(D_MODEL),
        out_shape=jax.ShapeDtypeStruct((B, T, D_MODEL), F32),
        compiler_params=_params(2),
        name="merge",
    )(x, attn, c, kt, va, g, wb, wa, wc, wm, wo)


def _mlp_kernel(h_ref, g_ref, wu_ref, wd_ref, gf_ref, y_ref):
    h = h_ref[0]
    hn = _rms_norm(h, g_ref[...]).astype(BF16)
    a = jnp.dot(hn, wu_ref[...], preferred_element_type=F32)
    a = jnp.square(jnp.maximum(a, 0.0)).astype(BF16)
    y = h + jnp.dot(a, wd_ref[...], preferred_element_type=F32)
    y_ref[0] = _rms_norm(y, gf_ref[...])


def _mlp(h, g, wu, wd, gf, tm):
    B, T, _ = h.shape
    const = lambda b, i: (0, 0)
    return pl.pallas_call(
        _mlp_kernel,
        grid=(B, T // tm),
        in_specs=[
            pl.BlockSpec((1, tm, D_MODEL), lambda b, i: (b, i, 0)),
            pl.BlockSpec((1, D_MODEL), const),
            pl.BlockSpec((D_MODEL, D_FF), const),
            pl.BlockSpec((D_FF, D_MODEL), const),
            pl.BlockSpec((1, D_MODEL), const),
        ],
        out_specs=pl.BlockSpec((1, tm, D_MODEL), lambda b, i: (b, i, 0)),
        out_shape=jax.ShapeDtypeStruct((B, T, D_MODEL), F32),
        compiler_params=_params(2),
        name="mlp",
    )(h, g, wu, wd, gf)


def _rope_tables(T):
    t = jnp.arange(T)
    inv_freq = ROPE_THETA ** (-jnp.arange(ROPE_PAIRS, dtype=F32) / ROPE_PAIRS)
    ang_r = (t // GRID_W).astype(F32)[:, None] * inv_freq[None, :]
    ang_c = (t % GRID_W).astype(F32)[:, None] * inv_freq[None, :]
    zero = jnp.zeros_like(ang_r)
    cos_h = jnp.concatenate([jnp.cos(ang_r), jnp.cos(ang_r), jnp.cos(ang_c), jnp.cos(ang_c)], -1)
    sup_h = jnp.concatenate([-jnp.sin(ang_r), zero, -jnp.sin(ang_c), zero], -1)
    sdn_h = jnp.concatenate([zero, jnp.sin(ang_r), zero, jnp.sin(ang_c)], -1)
    two = LANES // HEAD_DIM
    return jnp.tile(cos_h, (1, two)), jnp.tile(sup_h, (1, two)), jnp.tile(sdn_h, (1, two))


def _tiles(T):
    tm = min(512, T)
    tq = min(256, T)
    tk = min(1024, T)
    return tm, tq, tk


def _trunk(x, mem, p):
    T = x.shape[1]
    tm, tq, tk = _tiles(T)
    cos, sup, sdn = _rope_tables(T)
    kt, va = _mem_kv(mem, p["mem_norm"], p["w_mem_kv"])
    q, k, v, u = _in_proj(x, p["attn_norm"], p["w_a"], p["gq"], p["gk"], p["bd"], cos, sup, sdn, tm)
    attn = _flash(q, k, v, tq, tk)
    c = _conv(u, p["conv_w"], p["conv_b"], p["conv_ln_g"], p["conv_ln_b"], tm, 32)
    h = _merge(x, attn, c, kt, va, p["attn_norm"], p["w_b"], p["w_attn_o"], p["w_conv_o"],
               p["w_mem_o"], p["w_out"], tm)
    return _mlp(h, p["mlp_norm"], p["w_up"], p["w_down"], p["final_norm"], tm)


def kernel(x_prompt, x_sample, mem_prompt, mem_sample, attn_norm, w_in, q_norm, k_norm, w_attn_o,
           conv_w, conv_b, conv_ln_g, conv_ln_b, w_conv_o, mem_norm, w_mem_kv, w_mem_o, w_out,
           mlp_norm, w_up, w_down, final_norm):
    assert w_in.shape == (1, D_MODEL, IN_COLS)
    row = lambda a: a.reshape(1, -1).astype(F32)
    two = LANES // HEAD_DIM
    head_id = jnp.arange(LANES) // HEAD_DIM
    p = {
        "attn_norm": row(attn_norm[0]),
        "w_a": w_in[0, :, :OFF_XQ].astype(BF16),
        "w_b": w_in[0, :, OFF_XQ:].astype(BF16),
        "gq": row(jnp.tile(q_norm[0], two)),
        "gk": row(jnp.tile(k_norm[0], two)),
        "bd": (head_id[:, None] == head_id[None, :]).astype(BF16),
        "w_attn_o": w_attn_o[0].astype(BF16),
        "conv_w": conv_w[0].astype(F32),
        "conv_b": row(conv_b[0]),
        "conv_ln_g": row(conv_ln_g[0]),
        "conv_ln_b": row(conv_ln_b[0]),
        "w_conv_o": w_conv_o[0].astype(BF16),
        "mem_norm": row(mem_norm[0]),
        "w_mem_kv": w_mem_kv[0].astype(BF16),
        "w_mem_o": w_mem_o[0].astype(BF16),
        "w_out": w_out[0].astype(BF16),
        "mlp_norm": row(mlp_norm[0]),
        "w_up": w_up[0].astype(BF16),
        "w_down": w_down[0].astype(BF16),
        "final_norm": row(final_norm),
    }
    return (_trunk(x_prompt, mem_prompt, p), _trunk(x_sample, mem_sample, p))
```

```python
import functools

import jax
import jax.numpy as jnp
from jax import lax
from jax.experimental import pallas as pl
from jax.experimental.pallas import tpu as pltpu

F32 = jnp.float32
BF16 = jnp.bfloat16

D_MODEL = 1024
GRID_W = 64
N_HEADS = 8
N_KV_HEADS = 2
HEAD_DIM = 64
GQA_GROUP = N_HEADS // N_KV_HEADS
ATTN_W = N_HEADS * HEAD_DIM
KV_W = N_KV_HEADS * HEAD_DIM
GROUP_W = GQA_GROUP * HEAD_DIM
ROPE_THETA = 10000.0
ROPE_PAIRS = HEAD_DIM // 4
CONV_CH = 512
CONV_WIDTH = 31
CONV_PAD = CONV_WIDTH // 2
N_MEM = 256
MEM_HEADS = 4
MEM_HEAD_DIM = 128
MEM_W = MEM_HEADS * MEM_HEAD_DIM
N_BRANCH = 3
D_FF = 4 * D_MODEL
RMS_EPS = 1e-6
LN_EPS = 1e-5
OFF_Q = 0
OFF_K = OFF_Q + ATTN_W
OFF_V = OFF_K + KV_W
OFF_GLU = OFF_V + KV_W
OFF_XQ = OFF_GLU + 2 * CONV_CH
OFF_GATE = OFF_XQ + MEM_W
IN_COLS = OFF_GATE + N_BRANCH * D_MODEL

LANES = 128
CONV_HALO = 16
NEG_BIG = -1e30
LOG2_E = 1.4426950408889634
VMEM_LIMIT_BYTES = 56 * 1024 * 1024


def _params(n_grid_dims, flags=None):
    return pltpu.CompilerParams(
        dimension_semantics=("arbitrary",) * n_grid_dims,
        vmem_limit_bytes=VMEM_LIMIT_BYTES,
        flags=flags)


def _rms_norm(x, gain):
    ms = jnp.mean(x * x, axis=-1, keepdims=True)
    return x * lax.rsqrt(ms + RMS_EPS) * gain


def _mem_kv_kernel(mem_ref, g_ref, w_ref, kt_ref, va_ref):
    mn = _rms_norm(mem_ref[0], g_ref[...]).astype(BF16)
    kv = jnp.dot(mn, w_ref[...], preferred_element_type=F32)
    kt_ref[0] = kv[:, :MEM_W].T.astype(BF16)
    ones = jnp.ones((N_MEM, MEM_HEAD_DIM), BF16)
    for h in range(MEM_HEADS):
        lo = MEM_W + h * MEM_HEAD_DIM
        va_ref[0, h, :, :MEM_HEAD_DIM] = kv[:, lo:lo + MEM_HEAD_DIM].astype(BF16)
        va_ref[0, h, :, MEM_HEAD_DIM:] = ones


def _mem_kv(mem, g, w):
    B = mem.shape[0]
    return pl.pallas_call(
        _mem_kv_kernel,
        grid=(B,),
        in_specs=[
            pl.BlockSpec((1, N_MEM, D_MODEL), lambda b: (b, 0, 0)),
            pl.BlockSpec((1, D_MODEL), lambda b: (0, 0)),
            pl.BlockSpec((D_MODEL, 2 * MEM_W), lambda b: (0, 0)),
        ],
        out_specs=[
            pl.BlockSpec((1, MEM_W, N_MEM), lambda b: (b, 0, 0)),
            pl.BlockSpec((1, MEM_HEADS, N_MEM, 2 * MEM_HEAD_DIM), lambda b: (b, 0, 0, 0)),
        ],
        out_shape=[
            jax.ShapeDtypeStruct((B, MEM_W, N_MEM), BF16),
            jax.ShapeDtypeStruct((B, MEM_HEADS, N_MEM, 2 * MEM_HEAD_DIM), BF16),
        ],
        compiler_params=_params(1),
        name="mem_kv",
    )(mem, g, w)


def _head_norm_rope(z, gain, bd, cos, sin_up, sin_dn, scale):
    zz = z * z
    hi = zz.astype(BF16)
    lo = (zz - hi.astype(F32)).astype(BF16)
    ssq = (jnp.dot(hi, bd, preferred_element_type=F32)
           + jnp.dot(lo, bd, preferred_element_type=F32))
    y = z * lax.rsqrt(ssq * (1.0 / HEAD_DIM) + RMS_EPS) * gain
    up = pltpu.roll(y, LANES - ROPE_PAIRS, 1)
    dn = pltpu.roll(y, ROPE_PAIRS, 1)
    return (y * cos + up * sin_up + dn * sin_dn) * scale


def _in_proj_kernel(x_ref, g_ref, w_ref, gq_ref, gk_ref, bd_ref, cos_ref, sup_ref, sdn_ref,
                    q_ref, k_ref, v_ref, u_ref):
    xn = _rms_norm(x_ref[0], g_ref[...]).astype(BF16)
    z = jnp.dot(xn, w_ref[...], preferred_element_type=F32)
    bd = bd_ref[...]
    cos, sup, sdn = cos_ref[...], sup_ref[...], sdn_ref[...]
    q_scale = HEAD_DIM ** -0.5 * LOG2_E
    for c in range(ATTN_W // LANES):
        zc = z[:, OFF_Q + c * LANES:OFF_Q + (c + 1) * LANES]
        q_ref[0, :, c * LANES:(c + 1) * LANES] = _head_norm_rope(
            zc, gq_ref[...], bd, cos, sup, sdn, q_scale).astype(BF16)
    k_ref[0] = _head_norm_rope(z[:, OFF_K:OFF_V], gk_ref[...], bd, cos, sup, sdn, 1.0).astype(BF16)
    v_ref[0] = z[:, OFF_V:OFF_GLU].astype(BF16)
    a = z[:, OFF_GLU:OFF_GLU + CONV_CH]
    b = z[:, OFF_GLU + CONV_CH:OFF_XQ]
    u_ref[0] = (a * jax.nn.sigmoid(b)).astype(BF16)


def _in_proj(x, g, w, gq, gk, bd, cos, sup, sdn, tm):
    B, T, _ = x.shape
    const = lambda b, i: (0, 0)
    tab = pl.BlockSpec((tm, LANES), lambda b, i: (i, 0))
    return pl.pallas_call(
        _in_proj_kernel,
        grid=(B, T // tm),
        in_specs=[
            pl.BlockSpec((1, tm, D_MODEL), lambda b, i: (b, i, 0)),
            pl.BlockSpec((1, D_MODEL), const),
            pl.BlockSpec((D_MODEL, OFF_XQ), const),
            pl.BlockSpec((1, LANES), const),
            pl.BlockSpec((1, LANES), const),
            pl.BlockSpec((LANES, LANES), const),
            tab, tab, tab,
        ],
        out_specs=[
            pl.BlockSpec((1, tm, ATTN_W), lambda b, i: (b, i, 0)),
            pl.BlockSpec((1, tm, KV_W), lambda b, i: (b, i, 0)),
            pl.BlockSpec((1, tm, KV_W), lambda b, i: (b, i, 0)),
            pl.BlockSpec((1, tm, CONV_CH), lambda b, i: (b, i, 0)),
        ],
        out_shape=[
            jax.ShapeDtypeStruct((B, T, ATTN_W), BF16),
            jax.ShapeDtypeStruct((B, T, KV_W), BF16),
            jax.ShapeDtypeStruct((B, T, KV_W), BF16),
            jax.ShapeDtypeStruct((B, T, CONV_CH), BF16),
        ],
        compiler_params=_params(2),
        name="in_proj",
    )(x, g, w, gq, gk, bd, cos, sup, sdn)


def _flash_kernel(q_ref, k_ref, v_ref, o_ref, vt_ref, qp_ref, m_ref, acc_ref,
                  s_a, s_b, mx_a, mx_b, *, tk, tq, tc, n_kv):
    g = pl.program_id(1)
    j = pl.program_id(2)
    is_g0 = g == 0

    @pl.when(j == 0)
    def _():
        ones = jnp.ones((HEAD_DIM, tk), BF16)

        def build(c, carry):
            vc = v_ref[0, pl.ds(pl.multiple_of(c * tk, tk), tk), :].astype(F32)
            vct = vc.T
            top = jnp.where(is_g0, vct[:HEAD_DIM], vct[HEAD_DIM:])
            vt_ref[c, :HEAD_DIM, :] = top.astype(BF16)
            vt_ref[c, HEAD_DIM:, :] = ones
            return carry

        lax.fori_loop(0, n_kv, build, 0)

    qt = q_ref[0].astype(F32).T
    zeros = jnp.zeros((HEAD_DIM, tq), F32)
    for i in range(GQA_GROUP):
        qi = qt[i * HEAD_DIM:(i + 1) * HEAD_DIM]
        qp_ref[i, :HEAD_DIM, :] = jnp.where(is_g0, qi, zeros).astype(BF16)
        qp_ref[i, HEAD_DIM:, :] = jnp.where(is_g0, zeros, qi).astype(BF16)
    m_ref[...] = jnp.full(m_ref.shape, NEG_BIG, F32)
    acc_ref[...] = jnp.zeros(acc_ref.shape, F32)

    n_ch = tk // tc

    def qk_head(n, i, s_out, mx_out):
        base = pl.multiple_of(n * tk, tk)
        mi = None
        for c in range(n_ch):
            kc = k_ref[0, pl.ds(base + c * tc, tc), :]
            s = jnp.dot(kc, qp_ref[i], preferred_element_type=F32)
            s_out[i, c * tc:(c + 1) * tc, :] = s
            mc = jnp.max(s, axis=0, keepdims=True)
            mi = mc if mi is None else jnp.maximum(mi, mc)
        mx_out[i] = mi

    def softmax_pv_head(vtb, i, s_in, mx_in):
        m_old = m_ref[i]
        m_new = jnp.maximum(m_old, mx_in[i])
        p = jnp.concatenate(
            [jnp.exp2(s_in[i, c * tc:(c + 1) * tc, :] - m_new).astype(BF16) for c in range(n_ch)],
            axis=0)
        pv = jnp.dot(vtb, p, preferred_element_type=F32)
        acc_ref[i] = acc_ref[i] * jnp.exp2(m_old - m_new) + pv
        m_ref[i] = m_new

    def step(n, cur, nxt, do_qk):
        vtb = vt_ref[n]
        for i in range(GQA_GROUP):
            if do_qk:
                qk_head(n + 1, i, *nxt)
            softmax_pv_head(vtb, i, *cur)

    buf_a = (s_a, mx_a)
    buf_b = (s_b, mx_b)

    def pair(t, last):
        n = 2 * t
        step(n, buf_a, buf_b, True)
        step(n + 1, buf_b, buf_a, not last)

    for i in range(GQA_GROUP):
        qk_head(0, i, *buf_a)

    def loop_body(t, carry):
        pair(t, False)
        return carry

    lax.fori_loop(0, n_kv // 2 - 1, loop_body, 0)
    pair(n_kv // 2 - 1, True)

    outs = []
    for i in range(GQA_GROUP):
        a = acc_ref[i]
        outs.append(a[:HEAD_DIM] / a[HEAD_DIM:HEAD_DIM + 1])
    o_ref[0] = jnp.concatenate(outs, axis=0).T.astype(BF16)


def _flash(q, k, v, tq, tk):
    B, T, _ = q.shape
    n_kv = T // tk
    assert n_kv % 2 == 0
    tc = min(256, tk)
    kern = functools.partial(_flash_kernel, tk=tk, tq=tq, tc=tc, n_kv=n_kv)
    s_buf = pltpu.VMEM((GQA_GROUP, tk, tq), F32)
    row_buf = pltpu.VMEM((GQA_GROUP, 1, tq), F32)
    return pl.pallas_call(
        kern,
        grid=(B, N_KV_HEADS, T // tq),
        in_specs=[
            pl.BlockSpec((1, tq, GROUP_W), lambda b, g, j: (b, j, g)),
            pl.BlockSpec((1, T, KV_W), lambda b, g, j: (b, 0, 0)),
            pl.BlockSpec((1, T, KV_W), lambda b, g, j: (b, 0, 0)),
        ],
        out_specs=pl.BlockSpec((1, tq, GROUP_W), lambda b, g, j: (b, j, g)),
        out_shape=jax.ShapeDtypeStruct((B, T, ATTN_W), BF16),
        scratch_shapes=[
            pltpu.VMEM((n_kv, 2 * HEAD_DIM, tk), BF16),
            pltpu.VMEM((GQA_GROUP, 2 * HEAD_DIM, tq), BF16),
            row_buf,
            pltpu.VMEM((GQA_GROUP, 2 * HEAD_DIM, tq), F32),
            s_buf, s_buf, row_buf, row_buf,
        ],
        compiler_params=_params(3),
        name="flash",
    )(q, k, v)


def _conv_kernel(prev_ref, cur_ref, next_ref, w_ref, b_ref, lg_ref, lb_ref, o_ref, ext_ref,
                 *, tm, rows):
    i = pl.program_id(1)
    last = pl.num_programs(1) - 1
    halo = jnp.zeros((CONV_HALO, CONV_CH), F32)
    ext_ref[:CONV_HALO, :] = jnp.where(i > 0, prev_ref[0].astype(F32), halo)
    ext_ref[CONV_HALO:CONV_HALO + tm, :] = cur_ref[0].astype(F32)
    ext_ref[CONV_HALO + tm:, :] = jnp.where(i < last, next_ref[0].astype(F32), halo)
    win_rows = rows + 2 * CONV_HALO
    first = CONV_HALO - CONV_PAD

    def chunk(c, carry):
        r0 = pl.multiple_of(c * rows, rows)
        win = ext_ref[pl.ds(r0, win_rows), :]
        acc = jnp.zeros((rows, CONV_CH), F32) + b_ref[...]
        for r in range(8):
            wr = win if r == 0 else pltpu.roll(win, win_rows - r, 0)
            for a in range((2 * CONV_HALO) // 8):
                tap = 8 * a + r - first
                if 0 <= tap < CONV_WIDTH:
                    acc = acc + wr[8 * a:8 * a + rows] * w_ref[tap:tap + 1, :]
        mu = jnp.mean(acc, axis=-1, keepdims=True)
        xc = acc - mu
        var = jnp.mean(xc * xc, axis=-1, keepdims=True)
        y = xc * lax.rsqrt(var + LN_EPS) * lg_ref[...] + lb_ref[...]
        o_ref[0, pl.ds(r0, rows), :] = (y * jax.nn.sigmoid(y)).astype(BF16)
        return carry

    lax.fori_loop(0, tm // rows, chunk, 0)


def _conv(u, w, b, lg, lb, tm, rows):
    B, T, _ = u.shape
    hb = tm // CONV_HALO
    n_halo_blocks = T // CONV_HALO
    kern = functools.partial(_conv_kernel, tm=tm, rows=rows)
    const = lambda b_, i: (0, 0)
    return pl.pallas_call(
        kern,
        grid=(B, T // tm),
        in_specs=[
            pl.BlockSpec((1, CONV_HALO, CONV_CH),
                         lambda b_, i: (b_, jnp.maximum(i * hb - 1, 0), 0)),
            pl.BlockSpec((1, tm, CONV_CH), lambda b_, i: (b_, i, 0)),
            pl.BlockSpec((1, CONV_HALO, CONV_CH),
                         lambda b_, i: (b_, jnp.minimum((i + 1) * hb, n_halo_blocks - 1), 0)),
            pl.BlockSpec((CONV_WIDTH, CONV_CH), const),
            pl.BlockSpec((1, CONV_CH), const),
            pl.BlockSpec((1, CONV_CH), const),
            pl.BlockSpec((1, CONV_CH), const),
        ],
        out_specs=pl.BlockSpec((1, tm, CONV_CH), lambda b_, i: (b_, i, 0)),
        out_shape=jax.ShapeDtypeStruct((B, T, CONV_CH), BF16),
        scratch_shapes=[pltpu.VMEM((tm + 2 * CONV_HALO, CONV_CH), F32)],
        compiler_params=_params(2),
        name="conv",
    )(u, u, u, w, b, lg, lb)


def _merge_kernel(x_ref, attn_ref, c_ref, kt_ref, va_ref, g_ref, wb_ref, wa_ref, wc_ref, wm_ref,
                  wo_ref, h_ref):
    x = x_ref[0]
    xn = _rms_norm(x, g_ref[...]).astype(BF16)
    zb = jnp.dot(xn, wb_ref[...], preferred_element_type=F32)
    xq = zb[:, :MEM_W].astype(BF16)
    mem_scale = MEM_HEAD_DIM ** -0.5
    heads = []
    for h in range(MEM_HEADS):
        lo = h * MEM_HEAD_DIM
        s = jnp.dot(xq[:, lo:lo + MEM_HEAD_DIM], kt_ref[0, lo:lo + MEM_HEAD_DIM, :],
                    preferred_element_type=F32) * mem_scale
        p = jnp.exp(s - jnp.max(s, axis=-1, keepdims=True)).astype(BF16)
        oa = jnp.dot(p, va_ref[0, h], preferred_element_type=F32)
        heads.append(oa[:, :MEM_HEAD_DIM] / oa[:, MEM_HEAD_DIM:])
    mo = jnp.concatenate(heads, axis=-1).astype(BF16)
    attn_out = jnp.dot(attn_ref[0], wa_ref[...], preferred_element_type=F32)
    conv_out = jnp.dot(c_ref[0], wc_ref[...], preferred_element_type=F32)
    mem_out = jnp.dot(mo, wm_ref[...], preferred_element_type=F32)
    g0 = jax.nn.sigmoid(zb[:, MEM_W:MEM_W + D_MODEL])
    g1 = jax.nn.sigmoid(zb[:, MEM_W + D_MODEL:MEM_W + 2 * D_MODEL])
    g2 = jax.nn.sigmoid(zb[:, MEM_W + 2 * D_MODEL:])
    merged = (g0 * attn_out + g1 * conv_out + g2 * mem_out).astype(BF16)
    h_ref[0] = x + jnp.dot(merged, wo_ref[...], preferred_element_type=F32)


def _merge(x, attn, c, kt, va, g, wb, wa, wc, wm, wo, tm):
    B, T, _ = x.shape
    const = lambda b, i: (0, 0)
    tok = lambda w: pl.BlockSpec((1, tm, w), lambda b, i: (b, i, 0))
    return pl.pallas_call(
        _merge_kernel,
        grid=(B, T // tm),
        in_specs=[
            tok(D_MODEL), tok(ATTN_W), tok(CONV_CH),
            pl.BlockSpec((1, MEM_W, N_MEM), lambda b, i: (b, 0, 0)),
            pl.BlockSpec((1, MEM_HEADS, N_MEM, 2 * MEM_HEAD_DIM), lambda b, i: (b, 0, 0, 0)),
            pl.BlockSpec((1, D_MODEL), const),
            pl.BlockSpec((D_MODEL, MEM_W + N_BRANCH * D_MODEL), const),
            pl.BlockSpec((ATTN_W, D_MODEL), const),
            pl.BlockSpec((CONV_CH, D_MODEL), const),
            pl.BlockSpec((MEM_W, D_MODEL), const),
            pl.BlockSpec((D_MODEL, D_MODEL), const),
        ],
        out_specs=tok(D_MODEL),
        out_shape=jax.ShapeDtypeStruct((B, T, D_MODEL), F32),
        compiler_params=_params(2),
        name="merge",
    )(x, attn, c, kt, va, g, wb, wa, wc, wm, wo)


def _mlp_kernel(h_ref, g_ref, wu_ref, wd_ref, gf_ref, y_ref):
    h = h_ref[0]
    hn = _rms_norm(h, g_ref[...]).astype(BF16)
    a = jnp.dot(hn, wu_ref[...], preferred_element_type=F32)
    a = jnp.square(jnp.maximum(a, 0.0)).astype(BF16)
    y = h + jnp.dot(a, wd_ref[...], preferred_element_type=F32)
    y_ref[0] = _rms_norm(y, gf_ref[...])


def _mlp(h, g, wu, wd, gf, tm):
    B, T, _ = h.shape
    const = lambda b, i: (0, 0)
    return pl.pallas_call(
        _mlp_kernel,
        grid=(B, T // tm),
        in_specs=[
            pl.BlockSpec((1, tm, D_MODEL), lambda b, i: (b, i, 0)),
            pl.BlockSpec((1, D_MODEL), const),
            pl.BlockSpec((D_MODEL, D_FF), const),
            pl.BlockSpec((D_FF, D_MODEL), const),
            pl.BlockSpec((1, D_MODEL), const),
        ],
        out_specs=pl.BlockSpec((1, tm, D_MODEL), lambda b, i: (b, i, 0)),
        out_shape=jax.ShapeDtypeStruct((B, T, D_MODEL), F32),
        compiler_params=_params(2),
        name="mlp",
    )(h, g, wu, wd, gf)


def _rope_tables(T):
    t = jnp.arange(T)
    inv_freq = ROPE_THETA ** (-jnp.arange(ROPE_PAIRS, dtype=F32) / ROPE_PAIRS)
    ang_r = (t // GRID_W).astype(F32)[:, None] * inv_freq[None, :]
    ang_c = (t % GRID_W).astype(F32)[:, None] * inv_freq[None, :]
    zero = jnp.zeros_like(ang_r)
    cos_h = jnp.concatenate([jnp.cos(ang_r), jnp.cos(ang_r), jnp.cos(ang_c), jnp.cos(ang_c)], -1)
    sup_h = jnp.concatenate([-jnp.sin(ang_r), zero, -jnp.sin(ang_c), zero], -1)
    sdn_h = jnp.concatenate([zero, jnp.sin(ang_r), zero, jnp.sin(ang_c)], -1)
    two = LANES // HEAD_DIM
    return jnp.tile(cos_h, (1, two)), jnp.tile(sup_h, (1, two)), jnp.tile(sdn_h, (1, two))


def _tiles(T):
    tm = min(512, T)
    tq = min(256, T)
    tk = min(1024, T)
    return tm, tq, tk


def _trunk(x, mem, p):
    T = x.shape[1]
    tm, tq, tk = _tiles(T)
    cos, sup, sdn = _rope_tables(T)
    kt, va = _mem_kv(mem, p["mem_norm"], p["w_mem_kv"])
    q, k, v, u = _in_proj(x, p["attn_norm"], p["w_a"], p["gq"], p["gk"], p["bd"], cos, sup, sdn, tm)
    attn = _flash(q, k, v, tq, tk)
    c = _conv(u, p["conv_w"], p["conv_b"], p["conv_ln_g"], p["conv_ln_b"], tm, 32)
    h = _merge(x, attn, c, kt, va, p["attn_norm"], p["w_b"], p["w_attn_o"], p["w_conv_o"],
               p["w_mem_o"], p["w_out"], tm)
    return _mlp(h, p["mlp_norm"], p["w_up"], p["w_down"], p["final_norm"], tm)


def kernel(x_prompt, x_sample, mem_prompt, mem_sample, attn_norm, w_in, q_norm, k_norm, w_attn_o,
           conv_w, conv_b, conv_ln_g, conv_ln_b, w_conv_o, mem_norm, w_mem_kv, w_mem_o, w_out,
           mlp_norm, w_up, w_down, final_norm):
    assert w_in.shape == (1, D_MODEL, IN_COLS)
    row = lambda a: a.reshape(1, -1).astype(F32)
    two = LANES // HEAD_DIM
    head_id = jnp.arange(LANES) // HEAD_DIM
    p = {
        "attn_norm": row(attn_norm[0]),
        "w_a": w_in[0, :, :OFF_XQ].astype(BF16),
        "w_b": w_in[0, :, OFF_XQ:].astype(BF16),
        "gq": row(jnp.tile(q_norm[0], two)),
        "gk": row(jnp.tile(k_norm[0], two)),
        "bd": (head_id[:, None] == head_id[None, :]).astype(BF16),
        "w_attn_o": w_attn_o[0].astype(BF16),
        "conv_w": conv_w[0].astype(F32),
        "conv_b": row(conv_b[0]),
        "conv_ln_g": row(conv_ln_g[0]),
        "conv_ln_b": row(conv_ln_b[0]),
        "w_conv_o": w_conv_o[0].astype(BF16),
        "mem_norm": row(mem_norm[0]),
        "w_mem_kv": w_mem_kv[0].astype(BF16),
        "w_mem_o": w_mem_o[0].astype(BF16),
        "w_out": w_out[0].astype(BF16),
        "mlp_norm": row(mlp_norm[0]),
        "w_up": w_up[0].astype(BF16),
        "w_down": w_down[0].astype(BF16),
        "final_norm": row(final_norm),
    }
    return (_trunk(x_prompt, mem_prompt, p), _trunk(x_sample, mem_sample, p))
```

```python
import functools

import jax
import jax.numpy as jnp
from jax import lax
from jax.experimental import pallas as pl
from jax.experimental.pallas import tpu as pltpu

F32 = jnp.float32
BF16 = jnp.bfloat16

D_MODEL = 1024
GRID_W = 64
N_HEADS = 8
N_KV_HEADS = 2
HEAD_DIM = 64
GQA_GROUP = N_HEADS // N_KV_HEADS
ATTN_W = N_HEADS * HEAD_DIM
KV_W = N_KV_HEADS * HEAD_DIM
GROUP_W = GQA_GROUP * HEAD_DIM
ROPE_THETA = 10000.0
ROPE_PAIRS = HEAD_DIM // 4
CONV_CH = 512
CONV_WIDTH = 31
CONV_PAD = CONV_WIDTH // 2
N_MEM = 256
MEM_HEADS = 4
MEM_HEAD_DIM = 128
MEM_W = MEM_HEADS * MEM_HEAD_DIM
N_BRANCH = 3
D_FF = 4 * D_MODEL
RMS_EPS = 1e-6
LN_EPS = 1e-5
OFF_Q = 0
OFF_K = OFF_Q + ATTN_W
OFF_V = OFF_K + KV_W
OFF_GLU = OFF_V + KV_W
OFF_XQ = OFF_GLU + 2 * CONV_CH
OFF_GATE = OFF_XQ + MEM_W
IN_COLS = OFF_GATE + N_BRANCH * D_MODEL

LANES = 128
SUBLANES = 8
MXU_WIDTH = 256
CONV_HALO = 16
NEG_BIG = -1e30
LOG2_E = 1.4426950408889634
VMEM_LIMIT_BYTES = 56 * 1024 * 1024


def _params(n_grid_dims, flags=None):
    return pltpu.CompilerParams(
        dimension_semantics=("arbitrary",) * n_grid_dims,
        vmem_limit_bytes=VMEM_LIMIT_BYTES,
        flags=flags)


def _rms_norm(x, gain):
    ms = jnp.mean(x * x, axis=-1, keepdims=True)
    return x * lax.rsqrt(ms + RMS_EPS) * gain


def _mem_kv_kernel(mem_ref, g_ref, w_ref, kt_ref, va_ref):
    mn = _rms_norm(mem_ref[0], g_ref[...]).astype(BF16)
    kv = jnp.dot(mn, w_ref[...], preferred_element_type=F32)
    kt_ref[0] = kv[:, :MEM_W].T.astype(BF16)
    ones = jnp.ones((N_MEM, MEM_HEAD_DIM), BF16)
    for h in range(MEM_HEADS):
        lo = MEM_W + h * MEM_HEAD_DIM
        va_ref[0, h, :, :MEM_HEAD_DIM] = kv[:, lo:lo + MEM_HEAD_DIM].astype(BF16)
        va_ref[0, h, :, MEM_HEAD_DIM:] = ones


def _mem_kv(mem, g, w):
    B = mem.shape[0]
    return pl.pallas_call(
        _mem_kv_kernel,
        grid=(B,),
        in_specs=[
            pl.BlockSpec((1, N_MEM, D_MODEL), lambda b: (b, 0, 0)),
            pl.BlockSpec((1, D_MODEL), lambda b: (0, 0)),
            pl.BlockSpec((D_MODEL, 2 * MEM_W), lambda b: (0, 0)),
        ],
        out_specs=[
            pl.BlockSpec((1, MEM_W, N_MEM), lambda b: (b, 0, 0)),
            pl.BlockSpec((1, MEM_HEADS, N_MEM, 2 * MEM_HEAD_DIM), lambda b: (b, 0, 0, 0)),
        ],
        out_shape=[
            jax.ShapeDtypeStruct((B, MEM_W, N_MEM), BF16),
            jax.ShapeDtypeStruct((B, MEM_HEADS, N_MEM, 2 * MEM_HEAD_DIM), BF16),
        ],
        compiler_params=_params(1),
        name="mem_kv",
    )(mem, g, w)


def _head_norm_rope(z, gain, bd, cos, sin_up, sin_dn, scale):
    width = z.shape[1]
    rep = width // LANES
    wide = lambda t: t if rep == 1 else jnp.concatenate([t] * rep, axis=1)
    ssq = jnp.dot((z * z).astype(BF16), bd[:width, :width], preferred_element_type=F32)
    y = z * lax.rsqrt(ssq * (1.0 / HEAD_DIM) + RMS_EPS) * wide(gain)
    up = pltpu.roll(y, width - ROPE_PAIRS, 1)
    dn = pltpu.roll(y, ROPE_PAIRS, 1)
    return (y * wide(cos) + up * wide(sin_up) + dn * wide(sin_dn)) * scale


def _in_proj_kernel(x_ref, g_ref, w_ref, gq_ref, gk_ref, bd_ref, cos_ref, sup_ref, sdn_ref,
                    q_ref, k_ref, v_ref, u_ref):
    xn = _rms_norm(x_ref[0], g_ref[...]).astype(BF16)
    proj = lambda lo, hi: jnp.dot(xn, w_ref[:, lo:hi], preferred_element_type=F32)
    bd = bd_ref[...]
    cos, sup, sdn = cos_ref[...], sup_ref[...], sdn_ref[...]
    q_scale = HEAD_DIM ** -0.5 * LOG2_E
    zq = proj(OFF_Q, OFF_K)
    zkv = proj(OFF_K, OFF_GLU)
    for c in range(ATTN_W // MXU_WIDTH):
        cols = slice(c * MXU_WIDTH, (c + 1) * MXU_WIDTH)
        q_ref[0, :, cols] = _head_norm_rope(zq[:, cols], gq_ref[...], bd, cos, sup, sdn,
                                            q_scale).astype(BF16)
    za = proj(OFF_GLU, OFF_GLU + CONV_CH)
    k_ref[0] = _head_norm_rope(zkv[:, :KV_W], gk_ref[...], bd, cos, sup, sdn, 1.0).astype(BF16)
    v_ref[0] = zkv[:, KV_W:].astype(BF16)
    zb = proj(OFF_GLU + CONV_CH, OFF_XQ)
    u_ref[0] = (za * jax.nn.sigmoid(zb)).astype(BF16)


def _in_proj(x, g, w, gq, gk, bd, cos, sup, sdn, tm):
    B, T, _ = x.shape
    const = lambda b, i: (0, 0)
    tab = pl.BlockSpec((tm, LANES), lambda b, i: (i, 0))
    return pl.pallas_call(
        _in_proj_kernel,
        grid=(B, T // tm),
        in_specs=[
            pl.BlockSpec((1, tm, D_MODEL), lambda b, i: (b, i, 0)),
            pl.BlockSpec((1, D_MODEL), const),
            pl.BlockSpec((D_MODEL, OFF_XQ), const),
            pl.BlockSpec((1, LANES), const),
            pl.BlockSpec((1, LANES), const),
            pl.BlockSpec((MXU_WIDTH, MXU_WIDTH), const),
            tab, tab, tab,
        ],
        out_specs=[
            pl.BlockSpec((1, tm, ATTN_W), lambda b, i: (b, i, 0)),
            pl.BlockSpec((1, tm, KV_W), lambda b, i: (b, i, 0)),
            pl.BlockSpec((1, tm, KV_W), lambda b, i: (b, i, 0)),
            pl.BlockSpec((1, tm, CONV_CH), lambda b, i: (b, i, 0)),
        ],
        out_shape=[
            jax.ShapeDtypeStruct((B, T, ATTN_W), BF16),
            jax.ShapeDtypeStruct((B, T, KV_W), BF16),
            jax.ShapeDtypeStruct((B, T, KV_W), BF16),
            jax.ShapeDtypeStruct((B, T, CONV_CH), BF16),
        ],
        compiler_params=_params(2),
        name="in_proj",
    )(x, g, w, gq, gk, bd, cos, sup, sdn)


def _flash_kernel(q_ref, qn_ref, k_ref, v_ref, o_ref, vt_ref, qp_ref, m_ref, acc_ref,
                  s_a, s_b, mx_a, mx_b, *, tk, tq, tc, n_kv):
    g = pl.program_id(1)
    j = pl.program_id(2)
    is_g0 = g == 0
    n_ch = tk // tc

    def make_qp(src_ref, slot):
        qt = src_ref[0].astype(F32).T
        zeros = jnp.zeros((HEAD_DIM, tq), F32)
        for i in range(GQA_GROUP):
            qi = qt[i * HEAD_DIM:(i + 1) * HEAD_DIM]
            qp_ref[slot, i, :HEAD_DIM, :] = jnp.where(is_g0, qi, zeros).astype(BF16)
            qp_ref[slot, i, HEAD_DIM:, :] = jnp.where(is_g0, zeros, qi).astype(BF16)

    def qk_head(n, slot, i, s_out, mx_out):
        base = pl.multiple_of(n * tk, tk)
        mi = None
        for c in range(n_ch):
            kc = k_ref[0, pl.ds(base + c * tc, tc), :]
            s = jnp.dot(kc, qp_ref[slot, i], preferred_element_type=F32)
            s_out[i, c * tc:(c + 1) * tc, :] = s
            mc = jnp.max(s, axis=0, keepdims=True)
            mi = mc if mi is None else jnp.maximum(mi, mc)
        mx_out[i] = mi

    def softmax_pv_head(vtb, i, s_in, mx_in):
        m_old = m_ref[i]
        m_new = jnp.maximum(m_old, mx_in[i])
        p = jnp.concatenate(
            [jnp.exp2(s_in[i, c * tc:(c + 1) * tc, :] - m_new).astype(BF16) for c in range(n_ch)],
            axis=0)
        pv = jnp.dot(vtb, p, preferred_element_type=F32)
        acc_ref[i] = acc_ref[i] * jnp.exp2(m_old - m_new) + pv
        m_ref[i] = m_new

    buf_a = (s_a, mx_a)
    buf_b = (s_b, mx_b)

    @pl.when(j == 0)
    def _():
        ones = jnp.ones((HEAD_DIM, tk), BF16)

        def build(c, carry):
            vc = v_ref[0, pl.ds(pl.multiple_of(c * tk, tk), tk), :].astype(F32)
            vct = vc.T
            top = jnp.where(is_g0, vct[:HEAD_DIM], vct[HEAD_DIM:])
            vt_ref[c, :HEAD_DIM, :] = top.astype(BF16)
            vt_ref[c, HEAD_DIM:, :] = ones
            return carry

        lax.fori_loop(0, n_kv, build, 0)
        make_qp(q_ref, 0)
        for i in range(GQA_GROUP):
            qk_head(0, 0, i, *buf_a)

    @pl.when(j > 0)
    def _():
        qp_ref[0] = qp_ref[1]

    make_qp(qn_ref, 1)
    m_ref[...] = jnp.full(m_ref.shape, NEG_BIG, F32)
    acc_ref[...] = jnp.zeros(acc_ref.shape, F32)

    def step(n, cur, nxt, n_next, slot):
        vtb = vt_ref[n]
        for i in range(GQA_GROUP):
            qk_head(n_next, slot, i, *nxt)
            softmax_pv_head(vtb, i, *cur)

    def pair(t, wrap):
        n = 2 * t
        step(n, buf_a, buf_b, n + 1, 0)
        if wrap:
            step(n + 1, buf_b, buf_a, 0, 1)
        else:
            step(n + 1, buf_b, buf_a, n + 2, 0)

    def loop_body(t, carry):
        pair(t, False)
        return carry

    lax.fori_loop(0, n_kv // 2 - 1, loop_body, 0)
    pair(n_kv // 2 - 1, True)

    outs = []
    for i in range(GQA_GROUP):
        a = acc_ref[i]
        outs.append(a[:HEAD_DIM] / a[HEAD_DIM:HEAD_DIM + 1])
    o_ref[0] = jnp.concatenate(outs, axis=0).T.astype(BF16)


def _flash(q, k, v, tq, tk):
    B, T, _ = q.shape
    n_kv = T // tk
    n_q = T // tq
    assert n_kv % 2 == 0
    tc = min(256, tk)
    kern = functools.partial(_flash_kernel, tk=tk, tq=tq, tc=tc, n_kv=n_kv)
    s_buf = pltpu.VMEM((GQA_GROUP, tk, tq), F32)
    row_buf = pltpu.VMEM((GQA_GROUP, 1, tq), F32)
    return pl.pallas_call(
        kern,
        grid=(B, N_KV_HEADS, n_q),
        in_specs=[
            pl.BlockSpec((1, tq, GROUP_W), lambda b, g, j: (b, j, g)),
            pl.BlockSpec((1, tq, GROUP_W), lambda b, g, j: (b, jnp.minimum(j + 1, n_q - 1), g)),
            pl.BlockSpec((1, T, KV_W), lambda b, g, j: (b, 0, 0)),
            pl.BlockSpec((1, T, KV_W), lambda b, g, j: (b, 0, 0)),
        ],
        out_specs=pl.BlockSpec((1, tq, GROUP_W), lambda b, g, j: (b, j, g)),
        out_shape=jax.ShapeDtypeStruct((B, T, ATTN_W), BF16),
        scratch_shapes=[
            pltpu.VMEM((n_kv, 2 * HEAD_DIM, tk), BF16),
            pltpu.VMEM((2, GQA_GROUP, 2 * HEAD_DIM, tq), BF16),
            row_buf,
            pltpu.VMEM((GQA_GROUP, 2 * HEAD_DIM, tq), F32),
            s_buf, s_buf, row_buf, row_buf,
        ],
        compiler_params=_params(3),
        name="flash",
    )(q, q, k, v)


def _conv_kernel(prev_ref, cur_ref, next_ref, sel_ref, tail_ref, unsel_ref, w_ref, b_ref, lg_ref,
                 lb_ref, o_ref, a_ref, pre_ref, *, tm, slabs):
    i = pl.program_id(1)
    last = pl.num_programs(1) - 1
    seg = tm // SUBLANES
    n_hi = 2 * CONV_HALO * SUBLANES
    halo = jnp.zeros((CONV_HALO, CONV_CH), BF16)
    head = jnp.concatenate([jnp.where(i > 0, prev_ref[0], halo), cur_ref[0, :tm - CONV_HALO, :]],
                           axis=0)
    tail = jnp.concatenate([cur_ref[0, tm - CONV_HALO:, :], jnp.where(i < last, next_ref[0], halo)],
                           axis=0)
    a_lo = jnp.dot(sel_ref[...], head, preferred_element_type=F32)
    a_ref[:tm, :] = a_lo
    up = pltpu.roll(a_lo[:n_hi], n_hi - 1, 0)
    a_tail = jnp.dot(tail_ref[...], tail, preferred_element_type=F32)
    sub = lax.broadcasted_iota(jnp.int32, (n_hi, CONV_CH), 0) % SUBLANES
    a_ref[tm:, :] = jnp.where(sub == SUBLANES - 1, a_tail, up)
    first = CONV_HALO - CONV_PAD
    n_lt = CONV_CH // LANES

    def chunk(c, carry):
        p0 = c * slabs
        acc = [[b_ref[:, j * LANES:(j + 1) * LANES] + jnp.zeros((SUBLANES, LANES), F32)
                for j in range(n_lt)] for _ in range(slabs)]
        for tap in range(CONV_WIDTH):
            for j in range(n_lt):
                lanes = slice(j * LANES, (j + 1) * LANES)
                wt = w_ref[tap * SUBLANES:(tap + 1) * SUBLANES, lanes]
                for sl in range(slabs):
                    start = pl.multiple_of((p0 + sl + tap + first) * SUBLANES, SUBLANES)
                    acc[sl][j] = acc[sl][j] + a_ref[pl.ds(start, SUBLANES), lanes] * wt
        for sl in range(slabs):
            start = pl.multiple_of((p0 + sl) * SUBLANES, SUBLANES)
            pre_ref[pl.ds(start, SUBLANES), :] = jnp.concatenate(acc[sl], axis=1)
        return carry

    lax.fori_loop(0, seg // slabs, chunk, 0)
    acc = pre_ref[...]
    mu = jnp.mean(acc, axis=-1, keepdims=True)
    xc = acc - mu
    var = jnp.mean(xc * xc, axis=-1, keepdims=True)
    y = xc * lax.rsqrt(var + LN_EPS) * lg_ref[...] + lb_ref[...]
    y = (y * jax.nn.sigmoid(y)).astype(BF16)
    o_ref[0] = jnp.dot(unsel_ref[...], y, preferred_element_type=F32).astype(BF16)


def _conv_perms(tm):
    seg = tm // SUBLANES
    r = jnp.arange(tm)
    src = (r % SUBLANES) * seg + r // SUBLANES
    sel = (src[:, None] == jnp.arange(tm)[None, :]).astype(BF16)
    n_hi = 2 * CONV_HALO * SUBLANES
    rh = jnp.arange(n_hi)
    tail = ((rh % SUBLANES == SUBLANES - 1)[:, None]
            & (rh[:, None] // SUBLANES == jnp.arange(2 * CONV_HALO)[None, :])).astype(BF16)
    return sel, tail, sel.T


def _conv(u, w, b, lg, lb, tm, slabs):
    B, T, _ = u.shape
    hb = tm // CONV_HALO
    n_halo_blocks = T // CONV_HALO
    seg = tm // SUBLANES
    assert seg % slabs == 0 and seg >= 2 * CONV_HALO
    sel, tail, unsel = _conv_perms(tm)
    kern = functools.partial(_conv_kernel, tm=tm, slabs=slabs)
    const = lambda b_, i: (0, 0)
    return pl.pallas_call(
        kern,
        grid=(B, T // tm),
        in_specs=[
            pl.BlockSpec((1, CONV_HALO, CONV_CH),
                         lambda b_, i: (b_, jnp.maximum(i * hb - 1, 0), 0)),
            pl.BlockSpec((1, tm, CONV_CH), lambda b_, i: (b_, i, 0)),
            pl.BlockSpec((1, CONV_HALO, CONV_CH),
                         lambda b_, i: (b_, jnp.minimum((i + 1) * hb, n_halo_blocks - 1), 0)),
            pl.BlockSpec(sel.shape, const),
            pl.BlockSpec(tail.shape, const),
            pl.BlockSpec(unsel.shape, const),
            pl.BlockSpec((CONV_WIDTH * SUBLANES, CONV_CH), const),
            pl.BlockSpec((1, CONV_CH), const),
            pl.BlockSpec((1, CONV_CH), const),
            pl.BlockSpec((1, CONV_CH), const),
        ],
        out_specs=pl.BlockSpec((1, tm, CONV_CH), lambda b_, i: (b_, i, 0)),
        out_shape=jax.ShapeDtypeStruct((B, T, CONV_CH), BF16),
        scratch_shapes=[pltpu.VMEM(((seg + 2 * CONV_HALO) * SUBLANES, CONV_CH), F32),
                        pltpu.VMEM((tm, CONV_CH), F32)],
        compiler_params=_params(2),
        name="conv",
    )(u, u, u, sel, tail, unsel, jnp.repeat(w, SUBLANES, axis=0), b, lg, lb)


def _merge_kernel(x_ref, attn_ref, c_ref, kt_ref, va_ref, g_ref, wb_ref, wa_ref, wc_ref, wm_ref,
                  wo_ref, h_ref):
    x = x_ref[0]
    xn = _rms_norm(x, g_ref[...]).astype(BF16)
    zb = jnp.dot(xn, wb_ref[...], preferred_element_type=F32)
    xq = zb[:, :MEM_W].astype(BF16)
    mem_scale = MEM_HEAD_DIM ** -0.5
    heads = []
    for h in range(MEM_HEADS):
        lo = h * MEM_HEAD_DIM
        s = jnp.dot(xq[:, lo:lo + MEM_HEAD_DIM], kt_ref[0, lo:lo + MEM_HEAD_DIM, :],
                    preferred_element_type=F32) * mem_scale
        p = jnp.exp(s - jnp.max(s, axis=-1, keepdims=True)).astype(BF16)
        oa = jnp.dot(p, va_ref[0, h], preferred_element_type=F32)
        heads.append(oa[:, :MEM_HEAD_DIM] / oa[:, MEM_HEAD_DIM:])
    mo = jnp.concatenate(heads, axis=-1).astype(BF16)
    attn_out = jnp.dot(attn_ref[0], wa_ref[...], preferred_element_type=F32)
    conv_out = jnp.dot(c_ref[0], wc_ref[...], preferred_element_type=F32)
    mem_out = jnp.dot(mo, wm_ref[...], preferred_element_type=F32)
    g0 = jax.nn.sigmoid(zb[:, MEM_W:MEM_W + D_MODEL])
    g1 = jax.nn.sigmoid(zb[:, MEM_W + D_MODEL:MEM_W + 2 * D_MODEL])
    g2 = jax.nn.sigmoid(zb[:, MEM_W + 2 * D_MODEL:])
    merged = (g0 * attn_out + g1 * conv_out + g2 * mem_out).astype(BF16)
    h_ref[0] = x + jnp.dot(merged, wo_ref[...], preferred_element_type=F32)


def _merge(x, attn, c, kt, va, g, wb, wa, wc, wm, wo, tm):
    B, T, _ = x.shape
    const = lambda b, i: (0, 0)
    tok = lambda w: pl.BlockSpec((1, tm, w), lambda b, i: (b, i, 0))
    return pl.pallas_call(
        _merge_kernel,
        grid=(B, T // tm),
        in_specs=[
            tok(D_MODEL), tok(ATTN_W), tok(CONV_CH),
            pl.BlockSpec((1, MEM_W, N_MEM), lambda b, i: (b, 0, 0)),
            pl.BlockSpec((1, MEM_HEADS, N_MEM, 2 * MEM_HEAD_DIM), lambda b, i: (b, 0, 0, 0)),
            pl.BlockSpec((1, D_MODEL), const),
            pl.BlockSpec((D_MODEL, MEM_W + N_BRANCH * D_MODEL), const),
            pl.BlockSpec((ATTN_W, D_MODEL), const),
            pl.BlockSpec((CONV_CH, D_MODEL), const),
            pl.BlockSpec((MEM_W, D_MODEL), const),
            pl.BlockSpec((D_MODEL, D_MODEL), const),
        ],
        out_specs=tok(D_MODEL),
        out_shape=jax.ShapeDtypeStruct((B, T, D_MODEL), F32),
        compiler_params=_params(2),
        name="merge",
    )(x, attn, c, kt, va, g, wb, wa, wc, wm, wo)


def _mlp_kernel(h_ref, g_ref, wu_ref, wd_ref, gf_ref, y_ref):
    h = h_ref[0]
    hn = _rms_norm(h, g_ref[...]).astype(BF16)
    a = jnp.dot(hn, wu_ref[...], preferred_element_type=F32)
    a = jnp.square(jnp.maximum(a, 0.0)).astype(BF16)
    y = h + jnp.dot(a, wd_ref[...], preferred_element_type=F32)
    y_ref[0] = _rms_norm(y, gf_ref[...])


def _mlp(h, g, wu, wd, gf, tm):
    B, T, _ = h.shape
    const = lambda b, i: (0, 0)
    return pl.pallas_call(
        _mlp_kernel,
        grid=(B, T // tm),
        in_specs=[
            pl.BlockSpec((1, tm, D_MODEL), lambda b, i: (b, i, 0)),
            pl.BlockSpec((1, D_MODEL), const),
            pl.BlockSpec((D_MODEL, D_FF), const),
            pl.BlockSpec((D_FF, D_MODEL), const),
            pl.BlockSpec((1, D_MODEL), const),
        ],
        out_specs=pl.BlockSpec((1, tm, D_MODEL), lambda b, i: (b, i, 0)),
        out_shape=jax.ShapeDtypeStruct((B, T, D_MODEL), F32),
        compiler_params=_params(2),
        name="mlp",
    )(h, g, wu, wd, gf)


def _rope_tables(T):
    t = jnp.arange(T)
    inv_freq = ROPE_THETA ** (-jnp.arange(ROPE_PAIRS, dtype=F32) / ROPE_PAIRS)
    ang_r = (t // GRID_W).astype(F32)[:, None] * inv_freq[None, :]
    ang_c = (t % GRID_W).astype(F32)[:, None] * inv_freq[None, :]
    zero = jnp.zeros_like(ang_r)
    cos_h = jnp.concatenate([jnp.cos(ang_r), jnp.cos(ang_r), jnp.cos(ang_c), jnp.cos(ang_c)], -1)
    sup_h = jnp.concatenate([-jnp.sin(ang_r), zero, -jnp.sin(ang_c), zero], -1)
    sdn_h = jnp.concatenate([zero, jnp.sin(ang_r), zero, jnp.sin(ang_c)], -1)
    two = LANES // HEAD_DIM
    return jnp.tile(cos_h, (1, two)), jnp.tile(sup_h, (1, two)), jnp.tile(sdn_h, (1, two))


def _tiles(T):
    tm = min(512, T)
    tq = min(256, T)
    tk = min(1024, T)
    return tm, tq, tk


def _trunk(x, mem, p):
    T = x.shape[1]
    tm, tq, tk = _tiles(T)
    cos, sup, sdn = _rope_tables(T)
    kt, va = _mem_kv(mem, p["mem_norm"], p["w_mem_kv"])
    q, k, v, u = _in_proj(x, p["attn_norm"], p["w_a"], p["gq"], p["gk"], p["bd"], cos, sup, sdn, tm)
    attn = _flash(q, k, v, tq, tk)
    c = _conv(u, p["conv_w"], p["conv_b"], p["conv_ln_g"], p["conv_ln_b"], tm, 4)
    h = _merge(x, attn, c, kt, va, p["attn_norm"], p["w_b"], p["w_attn_o"], p["w_conv_o"],
               p["w_mem_o"], p["w_out"], tm)
    return _mlp(h, p["mlp_norm"], p["w_up"], p["w_down"], p["final_norm"], tm)


def kernel(x_prompt, x_sample, mem_prompt, mem_sample, attn_norm, w_in, q_norm, k_norm, w_attn_o,
           conv_w, conv_b, conv_ln_g, conv_ln_b, w_conv_o, mem_norm, w_mem_kv, w_mem_o, w_out,
           mlp_norm, w_up, w_down, final_norm):
    assert w_in.shape == (1, D_MODEL, IN_COLS)
    row = lambda a: a.reshape(1, -1).astype(F32)
    two = LANES // HEAD_DIM
    head_id = jnp.arange(MXU_WIDTH) // HEAD_DIM
    p = {
        "attn_norm": row(attn_norm[0]),
        "w_a": w_in[0, :, :OFF_XQ].astype(BF16),
        "w_b": w_in[0, :, OFF_XQ:].astype(BF16),
        "gq": row(jnp.tile(q_norm[0], two)),
        "gk": row(jnp.tile(k_norm[0], two)),
        "bd": (head_id[:, None] == head_id[None, :]).astype(BF16),
        "w_attn_o": w_attn_o[0].astype(BF16),
        "conv_w": conv_w[0].astype(F32),
        "conv_b": row(conv_b[0]),
        "conv_ln_g": row(conv_ln_g[0]),
        "conv_ln_b": row(conv_ln_b[0]),
        "w_conv_o": w_conv_o[0].astype(BF16),
        "mem_norm": row(mem_norm[0]),
        "w_mem_kv": w_mem_kv[0].astype(BF16),
        "w_mem_o": w_mem_o[0].astype(BF16),
        "w_out": w_out[0].astype(BF16),
        "mlp_norm": row(mlp_norm[0]),
        "w_up": w_up[0].astype(BF16),
        "w_down": w_down[0].astype(BF16),
        "final_norm": row(final_norm),
    }
    return (_trunk(x_prompt, mem_prompt, p), _trunk(x_sample, mem_sample, p))
```

```python
import functools

import jax
import jax.numpy as jnp
from jax import lax
from jax.experimental import pallas as pl
from jax.experimental.pallas import tpu as pltpu

F32 = jnp.float32
BF16 = jnp.bfloat16

D_MODEL = 1024
GRID_W = 64
N_HEADS = 8
N_KV_HEADS = 2
HEAD_DIM = 64
GQA_GROUP = N_HEADS // N_KV_HEADS
ATTN_W = N_HEADS * HEAD_DIM
KV_W = N_KV_HEADS * HEAD_DIM
GROUP_W = GQA_GROUP * HEAD_DIM
ROPE_THETA = 10000.0
ROPE_PAIRS = HEAD_DIM // 4
CONV_CH = 512
CONV_WIDTH = 31
CONV_PAD = CONV_WIDTH // 2
N_MEM = 256
MEM_HEADS = 4
MEM_HEAD_DIM = 128
MEM_W = MEM_HEADS * MEM_HEAD_DIM
N_BRANCH = 3
D_FF = 4 * D_MODEL
RMS_EPS = 1e-6
LN_EPS = 1e-5
OFF_Q = 0
OFF_K = OFF_Q + ATTN_W
OFF_V = OFF_K + KV_W
OFF_GLU = OFF_V + KV_W
OFF_XQ = OFF_GLU + 2 * CONV_CH
OFF_GATE = OFF_XQ + MEM_W
IN_COLS = OFF_GATE + N_BRANCH * D_MODEL

LANES = 128
SUBLANES = 8
MXU_WIDTH = 256
PV_ROWS = HEAD_DIM + 16
CONV_HALO = 16
NEG_BIG = -1e30
LOG2_E = 1.4426950408889634
VMEM_LIMIT_BYTES = 56 * 1024 * 1024


def _params(n_grid_dims, flags=None):
    return pltpu.CompilerParams(
        dimension_semantics=("arbitrary",) * n_grid_dims,
        vmem_limit_bytes=VMEM_LIMIT_BYTES,
        flags=flags)


def _rms_norm(x, gain):
    ms = jnp.mean(x * x, axis=-1, keepdims=True)
    return x * lax.rsqrt(ms + RMS_EPS) * gain


def _mem_kv_kernel(mem_ref, g_ref, w_ref, kt_ref, va_ref):
    mn = _rms_norm(mem_ref[0], g_ref[...]).astype(BF16)
    kv = jnp.dot(mn, w_ref[...], preferred_element_type=F32)
    kt_ref[0] = kv[:, :MEM_W].T.astype(BF16)
    ones = jnp.ones((N_MEM, MEM_HEAD_DIM), BF16)
    for h in range(MEM_HEADS):
        lo = MEM_W + h * MEM_HEAD_DIM
        va_ref[0, h, :, :MEM_HEAD_DIM] = kv[:, lo:lo + MEM_HEAD_DIM].astype(BF16)
        va_ref[0, h, :, MEM_HEAD_DIM:] = ones


def _mem_kv(mem, g, w):
    B = mem.shape[0]
    return pl.pallas_call(
        _mem_kv_kernel,
        grid=(B,),
        in_specs=[
            pl.BlockSpec((1, N_MEM, D_MODEL), lambda b: (b, 0, 0)),
            pl.BlockSpec((1, D_MODEL), lambda b: (0, 0)),
            pl.BlockSpec((D_MODEL, 2 * MEM_W), lambda b: (0, 0)),
        ],
        out_specs=[
            pl.BlockSpec((1, MEM_W, N_MEM), lambda b: (b, 0, 0)),
            pl.BlockSpec((1, MEM_HEADS, N_MEM, 2 * MEM_HEAD_DIM), lambda b: (b, 0, 0, 0)),
        ],
        out_shape=[
            jax.ShapeDtypeStruct((B, MEM_W, N_MEM), BF16),
            jax.ShapeDtypeStruct((B, MEM_HEADS, N_MEM, 2 * MEM_HEAD_DIM), BF16),
        ],
        compiler_params=_params(1),
        name="mem_kv",
    )(mem, g, w)


def _head_norm_rope(z, gain, bd, cos, sin_up, sin_dn, scale):
    width = z.shape[1]
    rep = width // LANES
    wide = lambda t: t if rep == 1 else jnp.concatenate([t] * rep, axis=1)
    ssq = jnp.dot((z * z).astype(BF16), bd[:width, :width], preferred_element_type=F32)
    y = z * lax.rsqrt(ssq * (1.0 / HEAD_DIM) + RMS_EPS) * wide(gain)
    up = pltpu.roll(y, width - ROPE_PAIRS, 1)
    dn = pltpu.roll(y, ROPE_PAIRS, 1)
    return (y * wide(cos) + up * wide(sin_up) + dn * wide(sin_dn)) * scale


def _in_proj_kernel(x_ref, g_ref, w_ref, gq_ref, gk_ref, bd_ref, cos_ref, sup_ref, sdn_ref,
                    q_ref, k_ref, v_ref, u_ref):
    xn = _rms_norm(x_ref[0], g_ref[...]).astype(BF16)
    proj = lambda lo, hi: jnp.dot(xn, w_ref[:, lo:hi], preferred_element_type=F32)
    bd = bd_ref[...]
    cos, sup, sdn = cos_ref[...], sup_ref[...], sdn_ref[...]
    q_scale = HEAD_DIM ** -0.5 * LOG2_E
    zq = proj(OFF_Q, OFF_K)
    zkv = proj(OFF_K, OFF_GLU)
    for c in range(ATTN_W // MXU_WIDTH):
        cols = slice(c * MXU_WIDTH, (c + 1) * MXU_WIDTH)
        q_ref[0, :, cols] = _head_norm_rope(zq[:, cols], gq_ref[...], bd, cos, sup, sdn,
                                            q_scale).astype(BF16)
    za = proj(OFF_GLU, OFF_GLU + CONV_CH)
    k_ref[0] = _head_norm_rope(zkv[:, :KV_W], gk_ref[...], bd, cos, sup, sdn, 1.0).astype(BF16)
    v_ref[0] = zkv[:, KV_W:].astype(BF16)
    zb = proj(OFF_GLU + CONV_CH, OFF_XQ)
    u_ref[0] = (za * jax.nn.sigmoid(zb)).astype(BF16)


def _in_proj(x, g, w, gq, gk, bd, cos, sup, sdn, tm):
    B, T, _ = x.shape
    const = lambda b, i: (0, 0)
    tab = pl.BlockSpec((tm, LANES), lambda b, i: (i, 0))
    return pl.pallas_call(
        _in_proj_kernel,
        grid=(B, T // tm),
        in_specs=[
            pl.BlockSpec((1, tm, D_MODEL), lambda b, i: (b, i, 0)),
            pl.BlockSpec((1, D_MODEL), const),
            pl.BlockSpec((D_MODEL, OFF_XQ), const),
            pl.BlockSpec((1, LANES), const),
            pl.BlockSpec((1, LANES), const),
            pl.BlockSpec((MXU_WIDTH, MXU_WIDTH), const),
            tab, tab, tab,
        ],
        out_specs=[
            pl.BlockSpec((1, tm, ATTN_W), lambda b, i: (b, i, 0)),
            pl.BlockSpec((1, tm, KV_W), lambda b, i: (b, i, 0)),
            pl.BlockSpec((1, tm, KV_W), lambda b, i: (b, i, 0)),
            pl.BlockSpec((1, tm, CONV_CH), lambda b, i: (b, i, 0)),
        ],
        out_shape=[
            jax.ShapeDtypeStruct((B, T, ATTN_W), BF16),
            jax.ShapeDtypeStruct((B, T, KV_W), BF16),
            jax.ShapeDtypeStruct((B, T, KV_W), BF16),
            jax.ShapeDtypeStruct((B, T, CONV_CH), BF16),
        ],
        compiler_params=_params(2),
        name="in_proj",
    )(x, g, w, gq, gk, bd, cos, sup, sdn)


def _flash_kernel(q_ref, qn_ref, k_ref, v_ref, o_ref, vt_ref, qp_ref, m_ref, acc_ref,
                  s_a, s_b, mx_a, mx_b, *, tk, tq, tc, n_kv):
    g = pl.program_id(1)
    j = pl.program_id(2)
    is_g0 = g == 0
    n_ch = tk // tc

    def make_qp(src_ref, slot):
        qt = src_ref[0].astype(F32).T
        zeros = jnp.zeros((HEAD_DIM, tq), F32)
        for i in range(GQA_GROUP):
            qi = qt[i * HEAD_DIM:(i + 1) * HEAD_DIM]
            qp_ref[slot, i, :HEAD_DIM, :] = jnp.where(is_g0, qi, zeros).astype(BF16)
            qp_ref[slot, i, HEAD_DIM:, :] = jnp.where(is_g0, zeros, qi).astype(BF16)

    def qk_head(n, slot, i, s_out, mx_out):
        base = pl.multiple_of(n * tk, tk)
        mi = None
        for c in range(n_ch):
            kc = k_ref[0, pl.ds(base + c * tc, tc), :]
            s = jnp.dot(kc, qp_ref[slot, i], preferred_element_type=F32)
            s_out[i, c * tc:(c + 1) * tc, :] = s
            mc = jnp.max(s, axis=0, keepdims=True)
            mi = mc if mi is None else jnp.maximum(mi, mc)
        mx_out[i] = mi

    def fused_head(n_next, slot, vtb, i, s_in, mx_in, s_out, mx_out):
        base = pl.multiple_of(n_next * tk, tk)
        m_old = m_ref[i]
        m_new = jnp.maximum(m_old, mx_in[i])
        mi = None
        pv = None
        for c in range(n_ch):
            rows = slice(c * tc, (c + 1) * tc)
            kc = k_ref[0, pl.ds(base + c * tc, tc), :]
            s = jnp.dot(kc, qp_ref[slot, i], preferred_element_type=F32)
            s_out[i, rows, :] = s
            mc = jnp.max(s, axis=0, keepdims=True)
            mi = mc if mi is None else jnp.maximum(mi, mc)
            p = jnp.exp2(s_in[i, rows, :] - m_new).astype(BF16)
            d = jnp.dot(vtb[:, rows], p, preferred_element_type=F32)
            pv = d if pv is None else pv + d
        mx_out[i] = mi
        acc_ref[i] = acc_ref[i] * jnp.exp2(m_old - m_new) + pv
        m_ref[i] = m_new

    buf_a = (s_a, mx_a)
    buf_b = (s_b, mx_b)

    @pl.when(j == 0)
    def _():
        ones = jnp.ones((PV_ROWS - HEAD_DIM, tk), BF16)

        def build(c, carry):
            vc = v_ref[0, pl.ds(pl.multiple_of(c * tk, tk), tk), :].astype(F32)
            vct = vc.T
            top = jnp.where(is_g0, vct[:HEAD_DIM], vct[HEAD_DIM:])
            vt_ref[c, :HEAD_DIM, :] = top.astype(BF16)
            vt_ref[c, HEAD_DIM:, :] = ones
            return carry

        lax.fori_loop(0, n_kv, build, 0)
        make_qp(q_ref, 0)
        for i in range(GQA_GROUP):
            qk_head(0, 0, i, *buf_a)

    @pl.when(j > 0)
    def _():
        qp_ref[0] = qp_ref[1]

    make_qp(qn_ref, 1)
    m_ref[...] = jnp.full(m_ref.shape, NEG_BIG, F32)
    acc_ref[...] = jnp.zeros(acc_ref.shape, F32)

    def step(n, cur, nxt, n_next, slot):
        vtb = vt_ref[n]
        for i in range(GQA_GROUP):
            fused_head(n_next, slot, vtb, i, *cur, *nxt)

    def pair(t, wrap):
        n = 2 * t
        step(n, buf_a, buf_b, n + 1, 0)
        if wrap:
            step(n + 1, buf_b, buf_a, 0, 1)
        else:
            step(n + 1, buf_b, buf_a, n + 2, 0)

    def loop_body(t, carry):
        pair(t, False)
        return carry

    lax.fori_loop(0, n_kv // 2 - 1, loop_body, 0)
    pair(n_kv // 2 - 1, True)

    outs = []
    for i in range(GQA_GROUP):
        a = acc_ref[i]
        outs.append(a[:HEAD_DIM] / a[HEAD_DIM:HEAD_DIM + 1])
    o_ref[0] = jnp.concatenate(outs, axis=0).T.astype(BF16)


def _flash(q, k, v, tq, tk):
    B, T, _ = q.shape
    n_kv = T // tk
    n_q = T // tq
    assert n_kv % 2 == 0
    tc = min(256, tk)
    kern = functools.partial(_flash_kernel, tk=tk, tq=tq, tc=tc, n_kv=n_kv)
    s_buf = pltpu.VMEM((GQA_GROUP, tk, tq), F32)
    row_buf = pltpu.VMEM((GQA_GROUP, 1, tq), F32)
    return pl.pallas_call(
        kern,
        grid=(B, N_KV_HEADS, n_q),
        in_specs=[
            pl.BlockSpec((1, tq, GROUP_W), lambda b, g, j: (b, j, g)),
            pl.BlockSpec((1, tq, GROUP_W), lambda b, g, j: (b, jnp.minimum(j + 1, n_q - 1), g)),
            pl.BlockSpec((1, T, KV_W), lambda b, g, j: (b, 0, 0)),
            pl.BlockSpec((1, T, KV_W), lambda b, g, j: (b, 0, 0)),
        ],
        out_specs=pl.BlockSpec((1, tq, GROUP_W), lambda b, g, j: (b, j, g)),
        out_shape=jax.ShapeDtypeStruct((B, T, ATTN_W), BF16),
        scratch_shapes=[
            pltpu.VMEM((n_kv, PV_ROWS, tk), BF16),
            pltpu.VMEM((2, GQA_GROUP, 2 * HEAD_DIM, tq), BF16),
            row_buf,
            pltpu.VMEM((GQA_GROUP, PV_ROWS, tq), F32),
            s_buf, s_buf, row_buf, row_buf,
        ],
        compiler_params=_params(3),
        name="flash",
    )(q, q, k, v)


def _conv_fill(i, last, prev_ref, cur_ref, next_ref, sel_ref, tail_ref, a_ref, tm):
    n_hi = 2 * CONV_HALO * SUBLANES
    halo = jnp.zeros((CONV_HALO, CONV_CH), BF16)
    head = jnp.concatenate([jnp.where(i > 0, prev_ref[0], halo), cur_ref[0, :tm - CONV_HALO, :]],
                           axis=0)
    tail = jnp.concatenate([cur_ref[0, tm - CONV_HALO:, :], jnp.where(i < last, next_ref[0], halo)],
                           axis=0)
    a_lo = jnp.dot(sel_ref[...], head, preferred_element_type=F32)
    a_ref[:tm, :] = a_lo
    up = pltpu.roll(a_lo[:n_hi], n_hi - 1, 0)
    a_tail = jnp.dot(tail_ref[...], tail, preferred_element_type=F32)
    sub = lax.broadcasted_iota(jnp.int32, (n_hi, CONV_CH), 0) % SUBLANES
    a_ref[tm:, :] = jnp.where(sub == SUBLANES - 1, a_tail, up)


def _conv_finish(a_ref, pre_ref, unsel_ref, w_ref, b_ref, lg_ref, lb_ref, tm, slabs):
    seg = tm // SUBLANES
    first = CONV_HALO - CONV_PAD
    n_lt = CONV_CH // LANES
    for p0 in range(0, seg, slabs):
        acc = [[b_ref[:, j * LANES:(j + 1) * LANES] + jnp.zeros((SUBLANES, LANES), F32)
                for j in range(n_lt)] for _ in range(slabs)]
        for tap in range(CONV_WIDTH):
            for j in range(n_lt):
                lanes = slice(j * LANES, (j + 1) * LANES)
                wt = w_ref[tap * SUBLANES:(tap + 1) * SUBLANES, lanes]
                for sl in range(slabs):
                    start = (p0 + sl + tap + first) * SUBLANES
                    acc[sl][j] = acc[sl][j] + a_ref[start:start + SUBLANES, lanes] * wt
        for sl in range(slabs):
            start = (p0 + sl) * SUBLANES
            pre_ref[start:start + SUBLANES, :] = jnp.concatenate(acc[sl], axis=1)
    acc = pre_ref[...]
    mu = jnp.mean(acc, axis=-1, keepdims=True)
    xc = acc - mu
    var = jnp.mean(xc * xc, axis=-1, keepdims=True)
    y = xc * lax.rsqrt(var + LN_EPS) * lg_ref[...] + lb_ref[...]
    y = (y * jax.nn.sigmoid(y)).astype(BF16)
    return jnp.dot(unsel_ref[...], y, preferred_element_type=F32).astype(BF16)


def _conv_perms(tm):
    seg = tm // SUBLANES
    r = jnp.arange(tm)
    src = (r % SUBLANES) * seg + r // SUBLANES
    sel = (src[:, None] == jnp.arange(tm)[None, :]).astype(BF16)
    n_hi = 2 * CONV_HALO * SUBLANES
    rh = jnp.arange(n_hi)
    tail = ((rh % SUBLANES == SUBLANES - 1)[:, None]
            & (rh[:, None] // SUBLANES == jnp.arange(2 * CONV_HALO)[None, :])).astype(BF16)
    return sel, tail, sel.T


def _merge_kernel(x_ref, attn_ref, up_ref, uc_ref, un_ref, kt_ref, va_ref, g_ref, wb_ref, wa_ref,
                  wc_ref, wm_ref, wo_ref, sel_ref, tail_ref, unsel_ref, cw_ref, cb_ref, lg_ref, lb_ref,
                  h_ref, a_ref, pre_ref, *, tm, slabs):
    i = pl.program_id(1)
    last = pl.num_programs(1) - 1
    x = x_ref[0]
    xn = _rms_norm(x, g_ref[...]).astype(BF16)
    xq = jnp.dot(xn, wb_ref[:, :MEM_W], preferred_element_type=F32).astype(BF16)
    mem_scale = MEM_HEAD_DIM ** -0.5
    scores = []
    for h in range(MEM_HEADS):
        lo = h * MEM_HEAD_DIM
        scores.append(jnp.dot(xq[:, lo:lo + MEM_HEAD_DIM], kt_ref[0, lo:lo + MEM_HEAD_DIM, :],
                              preferred_element_type=F32) * mem_scale)
    _conv_fill(i, last, up_ref, uc_ref, un_ref, sel_ref, tail_ref, a_ref, tm)
    attn_out = jnp.dot(attn_ref[0], wa_ref[...], preferred_element_type=F32)
    zg = jnp.dot(xn, wb_ref[:, MEM_W:], preferred_element_type=F32)
    heads = []
    for h in range(MEM_HEADS):
        s = scores[h]
        p = jnp.exp(s - jnp.max(s, axis=-1, keepdims=True)).astype(BF16)
        oa = jnp.dot(p, va_ref[0, h], preferred_element_type=F32)
        heads.append(oa[:, :MEM_HEAD_DIM] / oa[:, MEM_HEAD_DIM:])
    mo = jnp.concatenate(heads, axis=-1).astype(BF16)
    mem_out = jnp.dot(mo, wm_ref[...], preferred_element_type=F32)
    c = _conv_finish(a_ref, pre_ref, unsel_ref, cw_ref, cb_ref, lg_ref, lb_ref, tm, slabs)
    conv_out = jnp.dot(c, wc_ref[...], preferred_element_type=F32)
    g0 = jax.nn.sigmoid(zg[:, :D_MODEL])
    g1 = jax.nn.sigmoid(zg[:, D_MODEL:2 * D_MODEL])
    g2 = jax.nn.sigmoid(zg[:, 2 * D_MODEL:])
    merged = (g0 * attn_out + g1 * conv_out + g2 * mem_out).astype(BF16)
    h_ref[0] = x + jnp.dot(merged, wo_ref[...], preferred_element_type=F32)


def _merge(x, attn, u, kt, va, g, wb, wa, wc, wm, wo, cw, cb, lg, lb, tm, slabs):
    B, T, _ = x.shape
    hb = tm // CONV_HALO
    n_halo_blocks = T // CONV_HALO
    seg = tm // SUBLANES
    assert seg % slabs == 0 and seg >= 2 * CONV_HALO
    sel, tail, unsel = _conv_perms(tm)
    const = lambda b, i: (0, 0)
    tok = lambda w: pl.BlockSpec((1, tm, w), lambda b, i: (b, i, 0))
    row = lambda w: pl.BlockSpec((1, w), const)
    kern = functools.partial(_merge_kernel, tm=tm, slabs=slabs)
    return pl.pallas_call(
        kern,
        grid=(B, T // tm),
        in_specs=[
            tok(D_MODEL), tok(ATTN_W),
            pl.BlockSpec((1, CONV_HALO, CONV_CH),
                         lambda b, i: (b, jnp.maximum(i * hb - 1, 0), 0)),
            tok(CONV_CH),
            pl.BlockSpec((1, CONV_HALO, CONV_CH),
                         lambda b, i: (b, jnp.minimum((i + 1) * hb, n_halo_blocks - 1), 0)),
            pl.BlockSpec((1, MEM_W, N_MEM), lambda b, i: (b, 0, 0)),
            pl.BlockSpec((1, MEM_HEADS, N_MEM, 2 * MEM_HEAD_DIM), lambda b, i: (b, 0, 0, 0)),
            row(D_MODEL),
            pl.BlockSpec((D_MODEL, MEM_W + N_BRANCH * D_MODEL), const),
            pl.BlockSpec((ATTN_W, D_MODEL), const),
            pl.BlockSpec((CONV_CH, D_MODEL), const),
            pl.BlockSpec((MEM_W, D_MODEL), const),
            pl.BlockSpec((D_MODEL, D_MODEL), const),
            pl.BlockSpec(sel.shape, const),
            pl.BlockSpec(tail.shape, const),
            pl.BlockSpec(unsel.shape, const),
            pl.BlockSpec((CONV_WIDTH * SUBLANES, CONV_CH), const),
            row(CONV_CH), row(CONV_CH), row(CONV_CH),
        ],
        out_specs=tok(D_MODEL),
        out_shape=jax.ShapeDtypeStruct((B, T, D_MODEL), F32),
        scratch_shapes=[pltpu.VMEM(((seg + 2 * CONV_HALO) * SUBLANES, CONV_CH), F32),
                        pltpu.VMEM((tm, CONV_CH), F32)],
        compiler_params=_params(2),
        name="merge",
    )(x, attn, u, u, u, kt, va, g, wb, wa, wc, wm, wo, sel, tail, unsel,
      jnp.repeat(cw, SUBLANES, axis=0), cb, lg, lb)


def _mlp_kernel(h_ref, g_ref, wu_ref, wd_ref, gf_ref, y_ref):
    h = h_ref[0]
    hn = _rms_norm(h, g_ref[...]).astype(BF16)
    a = jnp.dot(hn, wu_ref[...], preferred_element_type=F32)
    a = jnp.square(jnp.maximum(a, 0.0)).astype(BF16)
    y = h + jnp.dot(a, wd_ref[...], preferred_element_type=F32)
    y_ref[0] = _rms_norm(y, gf_ref[...])


def _mlp(h, g, wu, wd, gf, tm):
    B, T, _ = h.shape
    const = lambda b, i: (0, 0)
    return pl.pallas_call(
        _mlp_kernel,
        grid=(B, T // tm),
        in_specs=[
            pl.BlockSpec((1, tm, D_MODEL), lambda b, i: (b, i, 0)),
            pl.BlockSpec((1, D_MODEL), const),
            pl.BlockSpec((D_MODEL, D_FF), const),
            pl.BlockSpec((D_FF, D_MODEL), const),
            pl.BlockSpec((1, D_MODEL), const),
        ],
        out_specs=pl.BlockSpec((1, tm, D_MODEL), lambda b, i: (b, i, 0)),
        out_shape=jax.ShapeDtypeStruct((B, T, D_MODEL), F32),
        compiler_params=_params(2),
        name="mlp",
    )(h, g, wu, wd, gf)


def _rope_tables(T):
    t = jnp.arange(T)
    inv_freq = ROPE_THETA ** (-jnp.arange(ROPE_PAIRS, dtype=F32) / ROPE_PAIRS)
    ang_r = (t // GRID_W).astype(F32)[:, None] * inv_freq[None, :]
    ang_c = (t % GRID_W).astype(F32)[:, None] * inv_freq[None, :]
    zero = jnp.zeros_like(ang_r)
    cos_h = jnp.concatenate([jnp.cos(ang_r), jnp.cos(ang_r), jnp.cos(ang_c), jnp.cos(ang_c)], -1)
    sup_h = jnp.concatenate([-jnp.sin(ang_r), zero, -jnp.sin(ang_c), zero], -1)
    sdn_h = jnp.concatenate([zero, jnp.sin(ang_r), zero, jnp.sin(ang_c)], -1)
    two = LANES // HEAD_DIM
    return jnp.tile(cos_h, (1, two)), jnp.tile(sup_h, (1, two)), jnp.tile(sdn_h, (1, two))


def _tiles(T):
    tm = min(512, T)
    tq = min(256, T)
    tk = min(1024, T)
    return tm, tq, tk


def _trunk(x, mem, p):
    T = x.shape[1]
    tm, tq, tk = _tiles(T)
    cos, sup, sdn = _rope_tables(T)
    kt, va = _mem_kv(mem, p["mem_norm"], p["w_mem_kv"])
    q, k, v, u = _in_proj(x, p["attn_norm"], p["w_a"], p["gq"], p["gk"], p["bd"], cos, sup, sdn, tm)
    attn = _flash(q, k, v, tq, tk)
    h = _merge(x, attn, u, kt, va, p["attn_norm"], p["w_b"], p["w_attn_o"], p["w_conv_o"],
               p["w_mem_o"], p["w_out"], p["conv_w"], p["conv_b"], p["conv_ln_g"], p["conv_ln_b"],
               tm, 4)
    return _mlp(h, p["mlp_norm"], p["w_up"], p["w_down"], p["final_norm"], tm)


def kernel(x_prompt, x_sample, mem_prompt, mem_sample, attn_norm, w_in, q_norm, k_norm, w_attn_o,
           conv_w, conv_b, conv_ln_g, conv_ln_b, w_conv_o, mem_norm, w_mem_kv, w_mem_o, w_out,
           mlp_norm, w_up, w_down, final_norm):
    assert w_in.shape == (1, D_MODEL, IN_COLS)
    row = lambda a: a.reshape(1, -1).astype(F32)
    two = LANES // HEAD_DIM
    head_id = jnp.arange(MXU_WIDTH) // HEAD_DIM
    p = {
        "attn_norm": row(attn_norm[0]),
        "w_a": w_in[0, :, :OFF_XQ].astype(BF16),
        "w_b": w_in[0, :, OFF_XQ:].astype(BF16),
        "gq": row(jnp.tile(q_norm[0], two)),
        "gk": row(jnp.tile(k_norm[0], two)),
        "bd": (head_id[:, None] == head_id[None, :]).astype(BF16),
        "w_attn_o": w_attn_o[0].astype(BF16),
        "conv_w": conv_w[0].astype(F32),
        "conv_b": row(conv_b[0]),
        "conv_ln_g": row(conv_ln_g[0]),
        "conv_ln_b": row(conv_ln_b[0]),
        "w_conv_o": w_conv_o[0].astype(BF16),
        "mem_norm": row(mem_norm[0]),
        "w_mem_kv": w_mem_kv[0].astype(BF16),
        "w_mem_o": w_mem_o[0].astype(BF16),
        "w_out": w_out[0].astype(BF16),
        "mlp_norm": row(mlp_norm[0]),
        "w_up": w_up[0].astype(BF16),
        "w_down": w_down[0].astype(BF16),
        "final_norm": row(final_norm),
    }
    return (_trunk(x_prompt, mem_prompt, p), _trunk(x_sample, mem_sample, p))
```

```python
import functools

import jax
import jax.numpy as jnp
from jax import lax
from jax.experimental import pallas as pl
from jax.experimental.pallas import tpu as pltpu

F32 = jnp.float32
BF16 = jnp.bfloat16

D_MODEL = 1024
GRID_W = 64
N_HEADS = 8
N_KV_HEADS = 2
HEAD_DIM = 64
GQA_GROUP = N_HEADS // N_KV_HEADS
ATTN_W = N_HEADS * HEAD_DIM
KV_W = N_KV_HEADS * HEAD_DIM
GROUP_W = GQA_GROUP * HEAD_DIM
ROPE_THETA = 10000.0
ROPE_PAIRS = HEAD_DIM // 4
CONV_CH = 512
CONV_WIDTH = 31
CONV_PAD = CONV_WIDTH // 2
N_MEM = 256
MEM_HEADS = 4
MEM_HEAD_DIM = 128
MEM_W = MEM_HEADS * MEM_HEAD_DIM
N_BRANCH = 3
D_FF = 4 * D_MODEL
RMS_EPS = 1e-6
LN_EPS = 1e-5
OFF_Q = 0
OFF_K = OFF_Q + ATTN_W
OFF_V = OFF_K + KV_W
OFF_GLU = OFF_V + KV_W
OFF_XQ = OFF_GLU + 2 * CONV_CH
OFF_GATE = OFF_XQ + MEM_W
IN_COLS = OFF_GATE + N_BRANCH * D_MODEL

LANES = 128
SUBLANES = 8
MXU_WIDTH = 256
PV_ROWS = HEAD_DIM + 16
CONV_HALO = 16
NEG_BIG = -1e30
LOG2_E = 1.4426950408889634
VMEM_LIMIT_BYTES = 56 * 1024 * 1024


def _params(n_grid_dims, flags=None):
    return pltpu.CompilerParams(
        dimension_semantics=("arbitrary",) * n_grid_dims,
        vmem_limit_bytes=VMEM_LIMIT_BYTES,
        flags=flags)


def _rms_norm(x, gain):
    ms = jnp.mean(x * x, axis=-1, keepdims=True)
    return x * lax.rsqrt(ms + RMS_EPS) * gain


def _mem_kv_kernel(mem_ref, g_ref, w_ref, kt_ref, va_ref):
    mn = _rms_norm(mem_ref[0], g_ref[...]).astype(BF16)
    kv = jnp.dot(mn, w_ref[...], preferred_element_type=F32)
    kt_ref[0] = kv[:, :MEM_W].T.astype(BF16)
    ones = jnp.ones((N_MEM, MEM_HEAD_DIM), BF16)
    for h in range(MEM_HEADS):
        lo = MEM_W + h * MEM_HEAD_DIM
        va_ref[0, h, :, :MEM_HEAD_DIM] = kv[:, lo:lo + MEM_HEAD_DIM].astype(BF16)
        va_ref[0, h, :, MEM_HEAD_DIM:] = ones


def _mem_kv(mem, g, w):
    B = mem.shape[0]
    return pl.pallas_call(
        _mem_kv_kernel,
        grid=(B,),
        in_specs=[
            pl.BlockSpec((1, N_MEM, D_MODEL), lambda b: (b, 0, 0)),
            pl.BlockSpec((1, D_MODEL), lambda b: (0, 0)),
            pl.BlockSpec((D_MODEL, 2 * MEM_W), lambda b: (0, 0)),
        ],
        out_specs=[
            pl.BlockSpec((1, MEM_W, N_MEM), lambda b: (b, 0, 0)),
            pl.BlockSpec((1, MEM_HEADS, N_MEM, 2 * MEM_HEAD_DIM), lambda b: (b, 0, 0, 0)),
        ],
        out_shape=[
            jax.ShapeDtypeStruct((B, MEM_W, N_MEM), BF16),
            jax.ShapeDtypeStruct((B, MEM_HEADS, N_MEM, 2 * MEM_HEAD_DIM), BF16),
        ],
        compiler_params=_params(1),
        name="mem_kv",
    )(mem, g, w)


def _head_norm_rope(z, gain, bd, cos, sin_up, sin_dn, scale):
    width = z.shape[1]
    rep = width // LANES
    wide = lambda t: t if rep == 1 else jnp.concatenate([t] * rep, axis=1)
    ssq = jnp.dot((z * z).astype(BF16), bd[:width, :width], preferred_element_type=F32)
    y = z * lax.rsqrt(ssq * (1.0 / HEAD_DIM) + RMS_EPS) * wide(gain)
    up = pltpu.roll(y, width - ROPE_PAIRS, 1)
    dn = pltpu.roll(y, ROPE_PAIRS, 1)
    return (y * wide(cos) + up * wide(sin_up) + dn * wide(sin_dn)) * scale


def _in_proj_kernel(x_ref, g_ref, w_ref, gq_ref, gk_ref, bd_ref, cos_ref, sup_ref, sdn_ref,
                    q_ref, k_ref, v_ref, u_ref):
    xn = _rms_norm(x_ref[0], g_ref[...]).astype(BF16)
    proj = lambda lo, hi: jnp.dot(xn, w_ref[:, lo:hi], preferred_element_type=F32)
    bd = bd_ref[...]
    cos, sup, sdn = cos_ref[...], sup_ref[...], sdn_ref[...]
    q_scale = HEAD_DIM ** -0.5 * LOG2_E
    zq = proj(OFF_Q, OFF_K)
    zkv = proj(OFF_K, OFF_GLU)
    for c in range(ATTN_W // MXU_WIDTH):
        cols = slice(c * MXU_WIDTH, (c + 1) * MXU_WIDTH)
        q_ref[0, :, cols] = _head_norm_rope(zq[:, cols], gq_ref[...], bd, cos, sup, sdn,
                                            q_scale).astype(BF16)
    za = proj(OFF_GLU, OFF_GLU + CONV_CH)
    k_ref[0] = _head_norm_rope(zkv[:, :KV_W], gk_ref[...], bd, cos, sup, sdn, 1.0).astype(BF16)
    v_ref[0] = zkv[:, KV_W:].astype(BF16)
    zb = proj(OFF_GLU + CONV_CH, OFF_XQ)
    u_ref[0] = (za * jax.nn.sigmoid(zb)).astype(BF16)


def _in_proj(x, g, w, gq, gk, bd, cos, sup, sdn, tm):
    B, T, _ = x.shape
    const = lambda b, i: (0, 0)
    tab = pl.BlockSpec((tm, LANES), lambda b, i: (i, 0))
    return pl.pallas_call(
        _in_proj_kernel,
        grid=(B, T // tm),
        in_specs=[
            pl.BlockSpec((1, tm, D_MODEL), lambda b, i: (b, i, 0)),
            pl.BlockSpec((1, D_MODEL), const),
            pl.BlockSpec((D_MODEL, OFF_XQ), const),
            pl.BlockSpec((1, LANES), const),
            pl.BlockSpec((1, LANES), const),
            pl.BlockSpec((MXU_WIDTH, MXU_WIDTH), const),
            tab, tab, tab,
        ],
        out_specs=[
            pl.BlockSpec((1, tm, ATTN_W), lambda b, i: (b, i, 0)),
            pl.BlockSpec((1, tm, KV_W), lambda b, i: (b, i, 0)),
            pl.BlockSpec((1, tm, KV_W), lambda b, i: (b, i, 0)),
            pl.BlockSpec((1, tm, CONV_CH), lambda b, i: (b, i, 0)),
        ],
        out_shape=[
            jax.ShapeDtypeStruct((B, T, ATTN_W), BF16),
            jax.ShapeDtypeStruct((B, T, KV_W), BF16),
            jax.ShapeDtypeStruct((B, T, KV_W), BF16),
            jax.ShapeDtypeStruct((B, T, CONV_CH), BF16),
        ],
        compiler_params=_params(2),
        name="in_proj",
    )(x, g, w, gq, gk, bd, cos, sup, sdn)


def _flash_kernel(q_ref, qn_ref, k_ref, v_ref, o_ref, vt_ref, qp_ref, m_ref, acc_ref,
                  s_a, s_b, mx_a, mx_b, *, tk, tq, tc, n_kv):
    g = pl.program_id(1)
    j = pl.program_id(2)
    is_g0 = g == 0
    n_ch = tk // tc

    def make_qp(src_ref, slot):
        qt = src_ref[0].astype(F32).T
        zeros = jnp.zeros((HEAD_DIM, tq), F32)
        for i in range(GQA_GROUP):
            qi = qt[i * HEAD_DIM:(i + 1) * HEAD_DIM]
            qp_ref[slot, i, :HEAD_DIM, :] = jnp.where(is_g0, qi, zeros).astype(BF16)
            qp_ref[slot, i, HEAD_DIM:, :] = jnp.where(is_g0, zeros, qi).astype(BF16)

    def qk_head(n, slot, i, s_out, mx_out):
        base = pl.multiple_of(n * tk, tk)
        mi = None
        for c in range(n_ch):
            kc = k_ref[0, pl.ds(base + c * tc, tc), :]
            s = jnp.dot(kc, qp_ref[slot, i], preferred_element_type=F32)
            s_out[i, c * tc:(c + 1) * tc, :] = s
            mc = jnp.max(s, axis=0, keepdims=True)
            mi = mc if mi is None else jnp.maximum(mi, mc)
        mx_out[i] = mi

    def fused_head(n_next, slot, vtb, i, s_in, mx_in, s_out, mx_out):
        base = pl.multiple_of(n_next * tk, tk)
        m_old = m_ref[i]
        m_new = jnp.maximum(m_old, mx_in[i])
        mi = None
        pv = None
        for c in range(n_ch):
            rows = slice(c * tc, (c + 1) * tc)
            kc = k_ref[0, pl.ds(base + c * tc, tc), :]
            s = jnp.dot(kc, qp_ref[slot, i], preferred_element_type=F32)
            s_out[i, rows, :] = s
            mc = jnp.max(s, axis=0, keepdims=True)
            mi = mc if mi is None else jnp.maximum(mi, mc)
            p = jnp.exp2(s_in[i, rows, :] - m_new).astype(BF16)
            d = jnp.dot(vtb[:, rows], p, preferred_element_type=F32)
            pv = d if pv is None else pv + d
        mx_out[i] = mi
        acc_ref[i] = acc_ref[i] * jnp.exp2(m_old - m_new) + pv
        m_ref[i] = m_new

    buf_a = (s_a, mx_a)
    buf_b = (s_b, mx_b)

    @pl.when(j == 0)
    def _():
        ones = jnp.ones((PV_ROWS - HEAD_DIM, tk), BF16)

        def build(c, carry):
            vc = v_ref[0, pl.ds(pl.multiple_of(c * tk, tk), tk), :].astype(F32)
            vct = vc.T
            top = jnp.where(is_g0, vct[:HEAD_DIM], vct[HEAD_DIM:])
            vt_ref[c, :HEAD_DIM, :] = top.astype(BF16)
            vt_ref[c, HEAD_DIM:, :] = ones
            return carry

        lax.fori_loop(0, n_kv, build, 0)
        make_qp(q_ref, 0)
        for i in range(GQA_GROUP):
            qk_head(0, 0, i, *buf_a)

    @pl.when(j > 0)
    def _():
        qp_ref[0] = qp_ref[1]

    make_qp(qn_ref, 1)
    m_ref[...] = jnp.full(m_ref.shape, NEG_BIG, F32)
    acc_ref[...] = jnp.zeros(acc_ref.shape, F32)

    def step(n, cur, nxt, n_next, slot):
        vtb = vt_ref[n]
        for i in range(GQA_GROUP):
            fused_head(n_next, slot, vtb, i, *cur, *nxt)

    def pair(t, wrap):
        n = 2 * t
        step(n, buf_a, buf_b, n + 1, 0)
        if wrap:
            step(n + 1, buf_b, buf_a, 0, 1)
        else:
            step(n + 1, buf_b, buf_a, n + 2, 0)

    def loop_body(t, carry):
        pair(t, False)
        return carry

    lax.fori_loop(0, n_kv // 2 - 1, loop_body, 0)
    pair(n_kv // 2 - 1, True)

    outs = []
    for i in range(GQA_GROUP):
        a = acc_ref[i]
        outs.append(a[:HEAD_DIM] / a[HEAD_DIM:HEAD_DIM + 1])
    o_ref[0] = jnp.concatenate(outs, axis=0).T.astype(BF16)


def _flash(q, k, v, tq, tk):
    B, T, _ = q.shape
    n_kv = T // tk
    n_q = T // tq
    assert n_kv % 2 == 0
    tc = min(256, tk)
    kern = functools.partial(_flash_kernel, tk=tk, tq=tq, tc=tc, n_kv=n_kv)
    s_buf = pltpu.VMEM((GQA_GROUP, tk, tq), F32)
    row_buf = pltpu.VMEM((GQA_GROUP, 1, tq), F32)
    return pl.pallas_call(
        kern,
        grid=(B, N_KV_HEADS, n_q),
        in_specs=[
            pl.BlockSpec((1, tq, GROUP_W), lambda b, g, j: (b, j, g)),
            pl.BlockSpec((1, tq, GROUP_W), lambda b, g, j: (b, jnp.minimum(j + 1, n_q - 1), g)),
            pl.BlockSpec((1, T, KV_W), lambda b, g, j: (b, 0, 0)),
            pl.BlockSpec((1, T, KV_W), lambda b, g, j: (b, 0, 0)),
        ],
        out_specs=pl.BlockSpec((1, tq, GROUP_W), lambda b, g, j: (b, j, g)),
        out_shape=jax.ShapeDtypeStruct((B, T, ATTN_W), BF16),
        scratch_shapes=[
            pltpu.VMEM((n_kv, PV_ROWS, tk), BF16),
            pltpu.VMEM((2, GQA_GROUP, 2 * HEAD_DIM, tq), BF16),
            row_buf,
            pltpu.VMEM((GQA_GROUP, PV_ROWS, tq), F32),
            s_buf, s_buf, row_buf, row_buf,
        ],
        compiler_params=_params(3),
        name="flash",
    )(q, q, k, v)


def _conv_fill(i, last, prev_ref, cur_ref, next_ref, sel_ref, tail_ref, a_ref, tm):
    n_hi = 2 * CONV_HALO * SUBLANES
    halo = jnp.zeros((CONV_HALO, CONV_CH), BF16)
    head = jnp.concatenate([jnp.where(i > 0, prev_ref[0], halo), cur_ref[0, :tm - CONV_HALO, :]],
                           axis=0)
    tail = jnp.concatenate([cur_ref[0, tm - CONV_HALO:, :], jnp.where(i < last, next_ref[0], halo)],
                           axis=0)
    a_lo = jnp.dot(sel_ref[...], head, preferred_element_type=F32)
    a_ref[:tm, :] = a_lo
    up = pltpu.roll(a_lo[:n_hi], n_hi - 1, 0)
    a_tail = jnp.dot(tail_ref[...], tail, preferred_element_type=F32)
    sub = lax.broadcasted_iota(jnp.int32, (n_hi, CONV_CH), 0) % SUBLANES
    a_ref[tm:, :] = jnp.where(sub == SUBLANES - 1, a_tail, up)


def _conv_finish(a_ref, pre_ref, unsel_ref, w_ref, b_ref, lg_ref, lb_ref, tm, slabs):
    seg = tm // SUBLANES
    first = CONV_HALO - CONV_PAD
    n_lt = CONV_CH // LANES
    for p0 in range(0, seg, slabs):
        acc = [[b_ref[:, j * LANES:(j + 1) * LANES] + jnp.zeros((SUBLANES, LANES), F32)
                for j in range(n_lt)] for _ in range(slabs)]
        for tap in range(CONV_WIDTH):
            for j in range(n_lt):
                lanes = slice(j * LANES, (j + 1) * LANES)
                wt = w_ref[tap * SUBLANES:(tap + 1) * SUBLANES, lanes]
                for sl in range(slabs):
                    start = (p0 + sl + tap + first) * SUBLANES
                    acc[sl][j] = acc[sl][j] + a_ref[start:start + SUBLANES, lanes] * wt
        for sl in range(slabs):
            start = (p0 + sl) * SUBLANES
            pre_ref[start:start + SUBLANES, :] = jnp.concatenate(acc[sl], axis=1)
    acc = pre_ref[...]
    mu = jnp.mean(acc, axis=-1, keepdims=True)
    xc = acc - mu
    var = jnp.mean(xc * xc, axis=-1, keepdims=True)
    y = xc * lax.rsqrt(var + LN_EPS) * lg_ref[...] + lb_ref[...]
    y = (y * jax.nn.sigmoid(y)).astype(BF16)
    return jnp.dot(unsel_ref[...], y, preferred_element_type=F32).astype(BF16)


def _conv_perms(tm):
    seg = tm // SUBLANES
    r = jnp.arange(tm)
    src = (r % SUBLANES) * seg + r // SUBLANES
    sel = (src[:, None] == jnp.arange(tm)[None, :]).astype(BF16)
    n_hi = 2 * CONV_HALO * SUBLANES
    rh = jnp.arange(n_hi)
    tail = ((rh % SUBLANES == SUBLANES - 1)[:, None]
            & (rh[:, None] // SUBLANES == jnp.arange(2 * CONV_HALO)[None, :])).astype(BF16)
    return sel, tail, sel.T


def _merge_kernel(x_ref, attn_ref, up_ref, uc_ref, un_ref, kt_ref, va_ref, g_ref, wb_ref, wa_ref,
                  wc_ref, wm_ref, wo_ref, sel_ref, tail_ref, unsel_ref, cw_ref, cb_ref, lg_ref, lb_ref,
                  h_ref, a_ref, pre_ref, *, tm, slabs):
    i = pl.program_id(1)
    last = pl.num_programs(1) - 1
    x = x_ref[0]
    xn = _rms_norm(x, g_ref[...]).astype(BF16)
    xq = jnp.dot(xn, wb_ref[:, :MEM_W], preferred_element_type=F32).astype(BF16)
    mem_scale = MEM_HEAD_DIM ** -0.5
    scores = []
    for h in range(MEM_HEADS):
        lo = h * MEM_HEAD_DIM
        scores.append(jnp.dot(xq[:, lo:lo + MEM_HEAD_DIM], kt_ref[0, lo:lo + MEM_HEAD_DIM, :],
                              preferred_element_type=F32) * mem_scale)
    _conv_fill(i, last, up_ref, uc_ref, un_ref, sel_ref, tail_ref, a_ref, tm)
    attn_out = jnp.dot(attn_ref[0], wa_ref[...], preferred_element_type=F32)
    zg = jnp.dot(xn, wb_ref[:, MEM_W:], preferred_element_type=F32)
    heads = []
    for h in range(MEM_HEADS):
        s = scores[h]
        p = jnp.exp(s - jnp.max(s, axis=-1, keepdims=True)).astype(BF16)
        oa = jnp.dot(p, va_ref[0, h], preferred_element_type=F32)
        heads.append(oa[:, :MEM_HEAD_DIM] / oa[:, MEM_HEAD_DIM:])
    mo = jnp.concatenate(heads, axis=-1).astype(BF16)
    mem_out = jnp.dot(mo, wm_ref[...], preferred_element_type=F32)
    c = _conv_finish(a_ref, pre_ref, unsel_ref, cw_ref, cb_ref, lg_ref, lb_ref, tm, slabs)
    conv_out = jnp.dot(c, wc_ref[...], preferred_element_type=F32)
    g0 = jax.nn.sigmoid(zg[:, :D_MODEL])
    g1 = jax.nn.sigmoid(zg[:, D_MODEL:2 * D_MODEL])
    g2 = jax.nn.sigmoid(zg[:, 2 * D_MODEL:])
    merged = (g0 * attn_out + g1 * conv_out + g2 * mem_out).astype(BF16)
    h_ref[0] = x + jnp.dot(merged, wo_ref[...], preferred_element_type=F32)


def _merge(x, attn, u, kt, va, g, wb, wa, wc, wm, wo, cw, cb, lg, lb, tm, slabs):
    B, T, _ = x.shape
    hb = tm // CONV_HALO
    n_halo_blocks = T // CONV_HALO
    seg = tm // SUBLANES
    assert seg % slabs == 0 and seg >= 2 * CONV_HALO
    sel, tail, unsel = _conv_perms(tm)
    const = lambda b, i: (0, 0)
    tok = lambda w: pl.BlockSpec((1, tm, w), lambda b, i: (b, i, 0))
    row = lambda w: pl.BlockSpec((1, w), const)
    kern = functools.partial(_merge_kernel, tm=tm, slabs=slabs)
    return pl.pallas_call(
        kern,
        grid=(B, T // tm),
        in_specs=[
            tok(D_MODEL), tok(ATTN_W),
            pl.BlockSpec((1, CONV_HALO, CONV_CH),
                         lambda b, i: (b, jnp.maximum(i * hb - 1, 0), 0)),
            tok(CONV_CH),
            pl.BlockSpec((1, CONV_HALO, CONV_CH),
                         lambda b, i: (b, jnp.minimum((i + 1) * hb, n_halo_blocks - 1), 0)),
            pl.BlockSpec((1, MEM_W, N_MEM), lambda b, i: (b, 0, 0)),
            pl.BlockSpec((1, MEM_HEADS, N_MEM, 2 * MEM_HEAD_DIM), lambda b, i: (b, 0, 0, 0)),
            row(D_MODEL),
            pl.BlockSpec((D_MODEL, MEM_W + N_BRANCH * D_MODEL), const),
            pl.BlockSpec((ATTN_W, D_MODEL), const),
            pl.BlockSpec((CONV_CH, D_MODEL), const),
            pl.BlockSpec((MEM_W, D_MODEL), const),
            pl.BlockSpec((D_MODEL, D_MODEL), const),
            pl.BlockSpec(sel.shape, const),
            pl.BlockSpec(tail.shape, const),
            pl.BlockSpec(unsel.shape, const),
            pl.BlockSpec((CONV_WIDTH * SUBLANES, CONV_CH), const),
            row(CONV_CH), row(CONV_CH), row(CONV_CH),
        ],
        out_specs=tok(D_MODEL),
        out_shape=jax.ShapeDtypeStruct((B, T, D_MODEL), F32),
        scratch_shapes=[pltpu.VMEM(((seg + 2 * CONV_HALO) * SUBLANES, CONV_CH), F32),
                        pltpu.VMEM((tm, CONV_CH), F32)],
        compiler_params=_params(2),
        name="merge",
    )(x, attn, u, u, u, kt, va, g, wb, wa, wc, wm, wo, sel, tail, unsel,
      jnp.repeat(cw, SUBLANES, axis=0), cb, lg, lb)


def _mlp_kernel(h_ref, g_ref, wu_ref, wd_ref, gf_ref, y_ref):
    h = h_ref[0]
    hn = _rms_norm(h, g_ref[...]).astype(BF16)
    a = jnp.dot(hn, wu_ref[...], preferred_element_type=F32)
    a = jnp.square(jnp.maximum(a, 0.0)).astype(BF16)
    y = h + jnp.dot(a, wd_ref[...], preferred_element_type=F32)
    y_ref[0] = _rms_norm(y, gf_ref[...])


def _mlp(h, g, wu, wd, gf, tm):
    B, T, _ = h.shape
    const = lambda b, i: (0, 0)
    return pl.pallas_call(
        _mlp_kernel,
        grid=(B, T // tm),
        in_specs=[
            pl.BlockSpec((1, tm, D_MODEL), lambda b, i: (b, i, 0)),
            pl.BlockSpec((1, D_MODEL), const),
            pl.BlockSpec((D_MODEL, D_FF), const),
            pl.BlockSpec((D_FF, D_MODEL), const),
            pl.BlockSpec((1, D_MODEL), const),
        ],
        out_specs=pl.BlockSpec((1, tm, D_MODEL), lambda b, i: (b, i, 0)),
        out_shape=jax.ShapeDtypeStruct((B, T, D_MODEL), F32),
        compiler_params=_params(2),
        name="mlp",
    )(h, g, wu, wd, gf)


def _rope_tables(T):
    t = jnp.arange(T)
    inv_freq = ROPE_THETA ** (-jnp.arange(ROPE_PAIRS, dtype=F32) / ROPE_PAIRS)
    ang_r = (t // GRID_W).astype(F32)[:, None] * inv_freq[None, :]
    ang_c = (t % GRID_W).astype(F32)[:, None] * inv_freq[None, :]
    zero = jnp.zeros_like(ang_r)
    cos_h = jnp.concatenate([jnp.cos(ang_r), jnp.cos(ang_r), jnp.cos(ang_c), jnp.cos(ang_c)], -1)
    sup_h = jnp.concatenate([-jnp.sin(ang_r), zero, -jnp.sin(ang_c), zero], -1)
    sdn_h = jnp.concatenate([zero, jnp.sin(ang_r), zero, jnp.sin(ang_c)], -1)
    two = LANES // HEAD_DIM
    return jnp.tile(cos_h, (1, two)), jnp.tile(sup_h, (1, two)), jnp.tile(sdn_h, (1, two))


def _tiles(T):
    tm = min(512, T)
    tq = min(512, T)
    tk = min(1024, T)
    return tm, tq, tk


def _trunk(x, mem, p):
    T = x.shape[1]
    tm, tq, tk = _tiles(T)
    cos, sup, sdn = _rope_tables(T)
    kt, va = _mem_kv(mem, p["mem_norm"], p["w_mem_kv"])
    q, k, v, u = _in_proj(x, p["attn_norm"], p["w_a"], p["gq"], p["gk"], p["bd"], cos, sup, sdn, tm)
    attn = _flash(q, k, v, tq, tk)
    h = _merge(x, attn, u, kt, va, p["attn_norm"], p["w_b"], p["w_attn_o"], p["w_conv_o"],
               p["w_mem_o"], p["w_out"], p["conv_w"], p["conv_b"], p["conv_ln_g"], p["conv_ln_b"],
               tm, 4)
    return _mlp(h, p["mlp_norm"], p["w_up"], p["w_down"], p["final_norm"], tm)


def kernel(x_prompt, x_sample, mem_prompt, mem_sample, attn_norm, w_in, q_norm, k_norm, w_attn_o,
           conv_w, conv_b, conv_ln_g, conv_ln_b, w_conv_o, mem_norm, w_mem_kv, w_mem_o, w_out,
           mlp_norm, w_up, w_down, final_norm):
    assert w_in.shape == (1, D_MODEL, IN_COLS)
    row = lambda a: a.reshape(1, -1).astype(F32)
    two = LANES // HEAD_DIM
    head_id = jnp.arange(MXU_WIDTH) // HEAD_DIM
    p = {
        "attn_norm": row(attn_norm[0]),
        "w_a": w_in[0, :, :OFF_XQ].astype(BF16),
        "w_b": w_in[0, :, OFF_XQ:].astype(BF16),
        "gq": row(jnp.tile(q_norm[0], two)),
        "gk": row(jnp.tile(k_norm[0], two)),
        "bd": (head_id[:, None] == head_id[None, :]).astype(BF16),
        "w_attn_o": w_attn_o[0].astype(BF16),
        "conv_w": conv_w[0].astype(F32),
        "conv_b": row(conv_b[0]),
        "conv_ln_g": row(conv_ln_g[0]),
        "conv_ln_b": row(conv_ln_b[0]),
        "w_conv_o": w_conv_o[0].astype(BF16),
        "mem_norm": row(mem_norm[0]),
        "w_mem_kv": w_mem_kv[0].astype(BF16),
        "w_mem_o": w_mem_o[0].astype(BF16),
        "w_out": w_out[0].astype(BF16),
        "mlp_norm": row(mlp_norm[0]),
        "w_up": w_up[0].astype(BF16),
        "w_down": w_down[0].astype(BF16),
        "final_norm": row(final_norm),
    }
    return (_trunk(x_prompt, mem_prompt, p), _trunk(x_sample, mem_sample, p))
```

```python
import functools

import jax
import jax.numpy as jnp
from jax import lax
from jax.experimental import pallas as pl
from jax.experimental.pallas import tpu as pltpu

F32 = jnp.float32
BF16 = jnp.bfloat16

D_MODEL = 1024
GRID_W = 64
N_HEADS = 8
N_KV_HEADS = 2
HEAD_DIM = 64
GQA_GROUP = N_HEADS // N_KV_HEADS
ATTN_W = N_HEADS * HEAD_DIM
KV_W = N_KV_HEADS * HEAD_DIM
GROUP_W = GQA_GROUP * HEAD_DIM
ROPE_THETA = 10000.0
ROPE_PAIRS = HEAD_DIM // 4
CONV_CH = 512
CONV_WIDTH = 31
CONV_PAD = CONV_WIDTH // 2
N_MEM = 256
MEM_HEADS = 4
MEM_HEAD_DIM = 128
MEM_W = MEM_HEADS * MEM_HEAD_DIM
N_BRANCH = 3
D_FF = 4 * D_MODEL
RMS_EPS = 1e-6
LN_EPS = 1e-5
OFF_Q = 0
OFF_K = OFF_Q + ATTN_W
OFF_V = OFF_K + KV_W
OFF_GLU = OFF_V + KV_W
OFF_XQ = OFF_GLU + 2 * CONV_CH
OFF_GATE = OFF_XQ + MEM_W
IN_COLS = OFF_GATE + N_BRANCH * D_MODEL

LANES = 128
SUBLANES = 8
MXU_WIDTH = 256
PV_ROWS = HEAD_DIM + 16
CONV_HALO = 16
NEG_BIG = -1e30
LOG2_E = 1.4426950408889634
VMEM_LIMIT_BYTES = 56 * 1024 * 1024


def _params(n_grid_dims, flags=None):
    return pltpu.CompilerParams(
        dimension_semantics=("arbitrary",) * n_grid_dims,
        vmem_limit_bytes=VMEM_LIMIT_BYTES,
        flags=flags)


def _rms_norm(x, gain):
    ms = jnp.mean(x * x, axis=-1, keepdims=True)
    return x * lax.rsqrt(ms + RMS_EPS) * gain


def _mem_kv_kernel(mem_ref, g_ref, w_ref, kt_ref, va_ref):
    mn = _rms_norm(mem_ref[0], g_ref[...]).astype(BF16)
    kv = jnp.dot(mn, w_ref[...], preferred_element_type=F32)
    kt_ref[0] = kv[:, :MEM_W].T.astype(BF16)
    ones = jnp.ones((N_MEM, MEM_HEAD_DIM), BF16)
    for h in range(MEM_HEADS):
        lo = MEM_W + h * MEM_HEAD_DIM
        va_ref[0, h, :, :MEM_HEAD_DIM] = kv[:, lo:lo + MEM_HEAD_DIM].astype(BF16)
        va_ref[0, h, :, MEM_HEAD_DIM:] = ones


def _mem_kv(mem, g, w):
    B = mem.shape[0]
    return pl.pallas_call(
        _mem_kv_kernel,
        grid=(B,),
        in_specs=[
            pl.BlockSpec((1, N_MEM, D_MODEL), lambda b: (b, 0, 0)),
            pl.BlockSpec((1, D_MODEL), lambda b: (0, 0)),
            pl.BlockSpec((D_MODEL, 2 * MEM_W), lambda b: (0, 0)),
        ],
        out_specs=[
            pl.BlockSpec((1, MEM_W, N_MEM), lambda b: (b, 0, 0)),
            pl.BlockSpec((1, MEM_HEADS, N_MEM, 2 * MEM_HEAD_DIM), lambda b: (b, 0, 0, 0)),
        ],
        out_shape=[
            jax.ShapeDtypeStruct((B, MEM_W, N_MEM), BF16),
            jax.ShapeDtypeStruct((B, MEM_HEADS, N_MEM, 2 * MEM_HEAD_DIM), BF16),
        ],
        compiler_params=_params(1),
        name="mem_kv",
    )(mem, g, w)


def _head_norm_rope(z, gain, bd, cos, sin_up, sin_dn, scale):
    width = z.shape[1]
    rep = width // LANES
    wide = lambda t: t if rep == 1 else jnp.concatenate([t] * rep, axis=1)
    ssq = jnp.dot((z * z).astype(BF16), bd[:width, :width], preferred_element_type=F32)
    y = z * lax.rsqrt(ssq * (1.0 / HEAD_DIM) + RMS_EPS) * wide(gain)
    up = pltpu.roll(y, width - ROPE_PAIRS, 1)
    dn = pltpu.roll(y, ROPE_PAIRS, 1)
    return (y * wide(cos) + up * wide(sin_up) + dn * wide(sin_dn)) * scale


def _in_proj_kernel(x_ref, g_ref, w_ref, gq_ref, gk_ref, bd_ref, cos_ref, sup_ref, sdn_ref,
                    q_ref, k_ref, v_ref, u_ref):
    xn = _rms_norm(x_ref[0], g_ref[...]).astype(BF16)
    proj = lambda lo, hi: jnp.dot(xn, w_ref[:, lo:hi], preferred_element_type=F32)
    bd = bd_ref[...]
    cos, sup, sdn = cos_ref[...], sup_ref[...], sdn_ref[...]
    q_scale = HEAD_DIM ** -0.5 * LOG2_E
    zq = proj(OFF_Q, OFF_K)
    zkv = proj(OFF_K, OFF_GLU)
    for c in range(ATTN_W // MXU_WIDTH):
        cols = slice(c * MXU_WIDTH, (c + 1) * MXU_WIDTH)
        q_ref[0, :, cols] = _head_norm_rope(zq[:, cols], gq_ref[...], bd, cos, sup, sdn,
                                            q_scale).astype(BF16)
    za = proj(OFF_GLU, OFF_GLU + CONV_CH)
    k_ref[0] = _head_norm_rope(zkv[:, :KV_W], gk_ref[...], bd, cos, sup, sdn, 1.0).astype(BF16)
    v_ref[0] = zkv[:, KV_W:].astype(BF16)
    zb = proj(OFF_GLU + CONV_CH, OFF_XQ)
    u_ref[0] = (za * jax.nn.sigmoid(zb)).astype(BF16)


def _in_proj(x, g, w, gq, gk, bd, cos, sup, sdn, tm):
    B, T, _ = x.shape
    const = lambda b, i: (0, 0)
    tab = pl.BlockSpec((tm, LANES), lambda b, i: (i, 0))
    return pl.pallas_call(
        _in_proj_kernel,
        grid=(B, T // tm),
        in_specs=[
            pl.BlockSpec((1, tm, D_MODEL), lambda b, i: (b, i, 0)),
            pl.BlockSpec((1, D_MODEL), const),
            pl.BlockSpec((D_MODEL, OFF_XQ), const),
            pl.BlockSpec((1, LANES), const),
            pl.BlockSpec((1, LANES), const),
            pl.BlockSpec((MXU_WIDTH, MXU_WIDTH), const),
            tab, tab, tab,
        ],
        out_specs=[
            pl.BlockSpec((1, tm, ATTN_W), lambda b, i: (b, i, 0)),
            pl.BlockSpec((1, tm, KV_W), lambda b, i: (b, i, 0)),
            pl.BlockSpec((1, tm, KV_W), lambda b, i: (b, i, 0)),
            pl.BlockSpec((1, tm, CONV_CH), lambda b, i: (b, i, 0)),
        ],
        out_shape=[
            jax.ShapeDtypeStruct((B, T, ATTN_W), BF16),
            jax.ShapeDtypeStruct((B, T, KV_W), BF16),
            jax.ShapeDtypeStruct((B, T, KV_W), BF16),
            jax.ShapeDtypeStruct((B, T, CONV_CH), BF16),
        ],
        compiler_params=_params(2),
        name="in_proj",
    )(x, g, w, gq, gk, bd, cos, sup, sdn)


def _flash_kernel(q_ref, qn_ref, k_ref, v_ref, o_ref, vt_ref, qp_ref, m_ref, acc_ref,
                  s_a, s_b, mx_a, mx_b, *, tk, tq, tc, n_kv):
    g = pl.program_id(1)
    j = pl.program_id(2)
    is_g0 = g == 0
    n_ch = tk // tc

    def make_qp(src_ref, slot):
        qt = src_ref[0].astype(F32).T
        zeros = jnp.zeros((HEAD_DIM, tq), F32)
        for i in range(GQA_GROUP):
            qi = qt[i * HEAD_DIM:(i + 1) * HEAD_DIM]
            qp_ref[slot, i, :HEAD_DIM, :] = jnp.where(is_g0, qi, zeros).astype(BF16)
            qp_ref[slot, i, HEAD_DIM:, :] = jnp.where(is_g0, zeros, qi).astype(BF16)

    def qk_head(n, slot, i, s_out, mx_out):
        base = pl.multiple_of(n * tk, tk)
        mi = None
        for c in range(n_ch):
            kc = k_ref[0, pl.ds(base + c * tc, tc), :]
            s = jnp.dot(kc, qp_ref[slot, i], preferred_element_type=F32)
            s_out[i, c * tc:(c + 1) * tc, :] = s
            mc = jnp.max(s, axis=0, keepdims=True)
            mi = mc if mi is None else jnp.maximum(mi, mc)
        mx_out[i] = mi

    def fused_head(n_next, slot, vtb, i, s_in, mx_in, s_out, mx_out):
        base = pl.multiple_of(n_next * tk, tk)
        m_old = m_ref[i]
        m_new = jnp.maximum(m_old, mx_in[i])
        mi = None
        pv = None
        for c in range(n_ch):
            rows = slice(c * tc, (c + 1) * tc)
            kc = k_ref[0, pl.ds(base + c * tc, tc), :]
            s = jnp.dot(kc, qp_ref[slot, i], preferred_element_type=F32)
            s_out[i, rows, :] = s
            mc = jnp.max(s, axis=0, keepdims=True)
            mi = mc if mi is None else jnp.maximum(mi, mc)
            p = jnp.exp2(s_in[i, rows, :] - m_new).astype(BF16)
            d = jnp.dot(vtb[:, rows], p, preferred_element_type=F32)
            pv = d if pv is None else pv + d
        mx_out[i] = mi
        acc_ref[i] = acc_ref[i] * jnp.exp2(m_old - m_new) + pv
        m_ref[i] = m_new

    buf_a = (s_a, mx_a)
    buf_b = (s_b, mx_b)

    @pl.when(j == 0)
    def _():
        ones = jnp.ones((PV_ROWS - HEAD_DIM, tk), BF16)

        def build(c, carry):
            vc = v_ref[0, pl.ds(pl.multiple_of(c * tk, tk), tk), :].astype(F32)
            vct = vc.T
            top = jnp.where(is_g0, vct[:HEAD_DIM], vct[HEAD_DIM:])
            vt_ref[c, :HEAD_DIM, :] = top.astype(BF16)
            vt_ref[c, HEAD_DIM:, :] = ones
            return carry

        lax.fori_loop(0, n_kv, build, 0)
        make_qp(q_ref, 0)
        for i in range(GQA_GROUP):
            qk_head(0, 0, i, *buf_a)

    @pl.when(j > 0)
    def _():
        qp_ref[0] = qp_ref[1]

    make_qp(qn_ref, 1)
    m_ref[...] = jnp.full(m_ref.shape, NEG_BIG, F32)
    acc_ref[...] = jnp.zeros(acc_ref.shape, F32)

    def step(n, cur, nxt, n_next, slot):
        vtb = vt_ref[n]
        for i in range(GQA_GROUP):
            fused_head(n_next, slot, vtb, i, *cur, *nxt)

    def pair(t, wrap):
        n = 2 * t
        step(n, buf_a, buf_b, n + 1, 0)
        if wrap:
            step(n + 1, buf_b, buf_a, 0, 1)
        else:
            step(n + 1, buf_b, buf_a, n + 2, 0)

    def loop_body(t, carry):
        pair(t, False)
        return carry

    lax.fori_loop(0, n_kv // 2 - 1, loop_body, 0)
    pair(n_kv // 2 - 1, True)

    outs = []
    for i in range(GQA_GROUP):
        a = acc_ref[i]
        outs.append(a[:HEAD_DIM] / a[HEAD_DIM:HEAD_DIM + 1])
    o_ref[0] = jnp.concatenate(outs, axis=0).T.astype(BF16)


def _flash(q, k, v, tq, tk):
    B, T, _ = q.shape
    n_kv = T // tk
    n_q = T // tq
    assert n_kv % 2 == 0
    tc = min(256, tk)
    kern = functools.partial(_flash_kernel, tk=tk, tq=tq, tc=tc, n_kv=n_kv)
    s_buf = pltpu.VMEM((GQA_GROUP, tk, tq), F32)
    row_buf = pltpu.VMEM((GQA_GROUP, 1, tq), F32)
    return pl.pallas_call(
        kern,
        grid=(B, N_KV_HEADS, n_q),
        in_specs=[
            pl.BlockSpec((1, tq, GROUP_W), lambda b, g, j: (b, j, g)),
            pl.BlockSpec((1, tq, GROUP_W), lambda b, g, j: (b, jnp.minimum(j + 1, n_q - 1), g)),
            pl.BlockSpec((1, T, KV_W), lambda b, g, j: (b, 0, 0)),
            pl.BlockSpec((1, T, KV_W), lambda b, g, j: (b, 0, 0)),
        ],
        out_specs=pl.BlockSpec((1, tq, GROUP_W), lambda b, g, j: (b, j, g)),
        out_shape=jax.ShapeDtypeStruct((B, T, ATTN_W), BF16),
        scratch_shapes=[
            pltpu.VMEM((n_kv, PV_ROWS, tk), BF16),
            pltpu.VMEM((2, GQA_GROUP, 2 * HEAD_DIM, tq), BF16),
            row_buf,
            pltpu.VMEM((GQA_GROUP, PV_ROWS, tq), F32),
            s_buf, s_buf, row_buf, row_buf,
        ],
        compiler_params=_params(3),
        name="flash",
    )(q, q, k, v)


def _conv_fill(i, last, prev_ref, cur_ref, next_ref, sel_ref, tail_ref, a_ref, tm):
    n_hi = 2 * CONV_HALO * SUBLANES
    halo = jnp.zeros((CONV_HALO, CONV_CH), BF16)
    head = jnp.concatenate([jnp.where(i > 0, prev_ref[0], halo), cur_ref[0, :tm - CONV_HALO, :]],
                           axis=0)
    tail = jnp.concatenate([cur_ref[0, tm - CONV_HALO:, :], jnp.where(i < last, next_ref[0], halo)],
                           axis=0)
    a_lo = jnp.dot(sel_ref[...], head, preferred_element_type=F32)
    a_ref[:tm, :] = a_lo
    up = pltpu.roll(a_lo[:n_hi], n_hi - 1, 0)
    a_tail = jnp.dot(tail_ref[...], tail, preferred_element_type=F32)
    sub = lax.broadcasted_iota(jnp.int32, (n_hi, CONV_CH), 0) % SUBLANES
    a_ref[tm:, :] = jnp.where(sub == SUBLANES - 1, a_tail, up)


def _conv_taps(a_ref, pre_ref, w_ref, b_ref, p0, slabs):
    first = CONV_HALO - CONV_PAD
    n_lt = CONV_CH // LANES
    acc = [[b_ref[:, j * LANES:(j + 1) * LANES] + jnp.zeros((SUBLANES, LANES), F32)
            for j in range(n_lt)] for _ in range(slabs)]
    for tap in range(CONV_WIDTH):
        for j in range(n_lt):
            lanes = slice(j * LANES, (j + 1) * LANES)
            wt = w_ref[tap * SUBLANES:(tap + 1) * SUBLANES, lanes]
            for sl in range(slabs):
                start = pl.multiple_of((p0 + sl + tap + first) * SUBLANES, SUBLANES)
                acc[sl][j] = acc[sl][j] + a_ref[pl.ds(start, SUBLANES), lanes] * wt
    for sl in range(slabs):
        start = pl.multiple_of((p0 + sl) * SUBLANES, SUBLANES)
        pre_ref[pl.ds(start, SUBLANES), :] = jnp.concatenate(acc[sl], axis=1)


def _conv_norm(pre_ref, unsel_ref, lg_ref, lb_ref):
    acc = pre_ref[...]
    mu = jnp.mean(acc, axis=-1, keepdims=True)
    xc = acc - mu
    var = jnp.mean(xc * xc, axis=-1, keepdims=True)
    y = xc * lax.rsqrt(var + LN_EPS) * lg_ref[...] + lb_ref[...]
    y = (y * jax.nn.sigmoid(y)).astype(BF16)
    return jnp.dot(unsel_ref[...], y, preferred_element_type=F32).astype(BF16)


def _conv_perms(tm):
    seg = tm // SUBLANES
    r = jnp.arange(tm)
    src = (r % SUBLANES) * seg + r // SUBLANES
    sel = (src[:, None] == jnp.arange(tm)[None, :]).astype(BF16)
    n_hi = 2 * CONV_HALO * SUBLANES
    rh = jnp.arange(n_hi)
    tail = ((rh % SUBLANES == SUBLANES - 1)[:, None]
            & (rh[:, None] // SUBLANES == jnp.arange(2 * CONV_HALO)[None, :])).astype(BF16)
    return sel, tail, sel.T


def _merge_kernel(x_ref, attn_ref, up_ref, uc_ref, un_ref, kt_ref, va_ref, g_ref, wq_ref, wg_ref,
                  wa_ref, wc_ref, wm_ref, wo_ref, sel_ref, tail_ref, unsel_ref, cw_ref, cb_ref, lg_ref,
                  lb_ref, h_ref, a_ref, pre_ref, xn_ref, zg_ref, ao_ref, mo_ref, *, tm, slabs, n_iter):
    i = pl.program_id(1)
    last = pl.num_programs(1) - 1
    xn = _rms_norm(x_ref[0], g_ref[...]).astype(BF16)
    xn_ref[...] = xn
    _conv_fill(i, last, up_ref, uc_ref, un_ref, sel_ref, tail_ref, a_ref, tm)
    xq = jnp.dot(xn, wq_ref[...], preferred_element_type=F32).astype(BF16)
    mem_scale = MEM_HEAD_DIM ** -0.5
    heads = []
    for h in range(MEM_HEADS):
        lo = h * MEM_HEAD_DIM
        s = jnp.dot(xq[:, lo:lo + MEM_HEAD_DIM], kt_ref[0, lo:lo + MEM_HEAD_DIM, :],
                    preferred_element_type=F32) * mem_scale
        p = jnp.exp(s - jnp.max(s, axis=-1, keepdims=True)).astype(BF16)
        oa = jnp.dot(p, va_ref[0, h], preferred_element_type=F32)
        heads.append(oa[:, :MEM_HEAD_DIM] / oa[:, MEM_HEAD_DIM:])
    mo = jnp.concatenate(heads, axis=-1).astype(BF16)
    mo_ref[...] = jnp.dot(mo, wm_ref[...], preferred_element_type=F32)

    gate_tiles = zg_ref.shape[0] // n_iter
    attn_tiles = ao_ref.shape[0] // n_iter
    conv_chunks = (tm // SUBLANES) // (slabs * n_iter)
    per = conv_chunks // (gate_tiles + attn_tiles)

    def body(k, carry):
        xk = xn_ref[...]
        cc = 0
        for t in range(gate_tiles):
            for _ in range(per):
                _conv_taps(a_ref, pre_ref, cw_ref, cb_ref, (k * conv_chunks + cc) * slabs, slabs)
                cc += 1
            zg_ref[k * gate_tiles + t] = jax.nn.sigmoid(
                jnp.dot(xk, wg_ref[k * gate_tiles + t], preferred_element_type=F32))
        for t in range(attn_tiles):
            for _ in range(per):
                _conv_taps(a_ref, pre_ref, cw_ref, cb_ref, (k * conv_chunks + cc) * slabs, slabs)
                cc += 1
            ao_ref[k * attn_tiles + t] = jnp.dot(attn_ref[0], wa_ref[k * attn_tiles + t],
                                                 preferred_element_type=F32)
        return carry

    lax.fori_loop(0, n_iter, body, 0)

    c = _conv_norm(pre_ref, unsel_ref, lg_ref, lb_ref)
    conv_out = jnp.dot(c, wc_ref[...], preferred_element_type=F32)
    per_branch = D_MODEL // MXU_WIDTH
    gate = lambda b: jnp.concatenate(
        [zg_ref[b * per_branch + t] for t in range(per_branch)], axis=1)
    attn_out = jnp.concatenate([ao_ref[t] for t in range(per_branch)], axis=1)
    merged = (gate(0) * attn_out + gate(1) * conv_out + gate(2) * mo_ref[...]).astype(BF16)
    h_ref[0] = x_ref[0] + jnp.dot(merged, wo_ref[...], preferred_element_type=F32)


def _col_tiles(w):
    k, n = w.shape
    return w.reshape(k, n // MXU_WIDTH, MXU_WIDTH).transpose(1, 0, 2)


def _merge(x, attn, u, kt, va, g, wb, wa, wc, wm, wo, cw, cb, lg, lb, tm, slabs):
    B, T, _ = x.shape
    hb = tm // CONV_HALO
    n_halo_blocks = T // CONV_HALO
    seg = tm // SUBLANES
    n_iter = 4
    sel, tail, unsel = _conv_perms(tm)
    wq = wb[:, :MEM_W]
    wg = _col_tiles(wb[:, MEM_W:])
    wa_t = _col_tiles(wa)
    dots_per_iter = (wg.shape[0] + wa_t.shape[0]) // n_iter
    assert seg % (slabs * n_iter * dots_per_iter) == 0 and seg >= 2 * CONV_HALO
    const = lambda b, i: (0, 0)
    const3 = lambda b, i: (0, 0, 0)
    tok = lambda w: pl.BlockSpec((1, tm, w), lambda b, i: (b, i, 0))
    row = lambda w: pl.BlockSpec((1, w), const)
    kern = functools.partial(_merge_kernel, tm=tm, slabs=slabs, n_iter=n_iter)
    return pl.pallas_call(
        kern,
        grid=(B, T // tm),
        in_specs=[
            tok(D_MODEL), tok(ATTN_W),
            pl.BlockSpec((1, CONV_HALO, CONV_CH),
                         lambda b, i: (b, jnp.maximum(i * hb - 1, 0), 0)),
            tok(CONV_CH),
            pl.BlockSpec((1, CONV_HALO, CONV_CH),
                         lambda b, i: (b, jnp.minimum((i + 1) * hb, n_halo_blocks - 1), 0)),
            pl.BlockSpec((1, MEM_W, N_MEM), lambda b, i: (b, 0, 0)),
            pl.BlockSpec((1, MEM_HEADS, N_MEM, 2 * MEM_HEAD_DIM), lambda b, i: (b, 0, 0, 0)),
            row(D_MODEL),
            pl.BlockSpec(wq.shape, const),
            pl.BlockSpec(wg.shape, const3),
            pl.BlockSpec(wa_t.shape, const3),
            pl.BlockSpec((CONV_CH, D_MODEL), const),
            pl.BlockSpec((MEM_W, D_MODEL), const),
            pl.BlockSpec((D_MODEL, D_MODEL), const),
            pl.BlockSpec(sel.shape, const),
            pl.BlockSpec(tail.shape, const),
            pl.BlockSpec(unsel.shape, const),
            pl.BlockSpec((CONV_WIDTH * SUBLANES, CONV_CH), const),
            row(CONV_CH), row(CONV_CH), row(CONV_CH),
        ],
        out_specs=tok(D_MODEL),
        out_shape=jax.ShapeDtypeStruct((B, T, D_MODEL), F32),
        scratch_shapes=[pltpu.VMEM(((seg + 2 * CONV_HALO) * SUBLANES, CONV_CH), F32),
                        pltpu.VMEM((tm, CONV_CH), F32),
                        pltpu.VMEM((tm, D_MODEL), BF16),
                        pltpu.VMEM((wg.shape[0], tm, MXU_WIDTH), F32),
                        pltpu.VMEM((wa_t.shape[0], tm, MXU_WIDTH), F32),
                        pltpu.VMEM((tm, D_MODEL), F32)],
        compiler_params=_params(2),
        name="merge",
    )(x, attn, u, u, u, kt, va, g, wq, wg, wa_t, wc, wm, wo, sel, tail, unsel,
      jnp.repeat(cw, SUBLANES, axis=0), cb, lg, lb)


def _mlp_kernel(h_ref, g_ref, wu_ref, wd_ref, gf_ref, y_ref):
    h = h_ref[0]
    hn = _rms_norm(h, g_ref[...]).astype(BF16)
    a = jnp.dot(hn, wu_ref[...], preferred_element_type=F32)
    a = jnp.square(jnp.maximum(a, 0.0)).astype(BF16)
    y = h + jnp.dot(a, wd_ref[...], preferred_element_type=F32)
    y_ref[0] = _rms_norm(y, gf_ref[...])


def _mlp(h, g, wu, wd, gf, tm):
    B, T, _ = h.shape
    const = lambda b, i: (0, 0)
    return pl.pallas_call(
        _mlp_kernel,
        grid=(B, T // tm),
        in_specs=[
            pl.BlockSpec((1, tm, D_MODEL), lambda b, i: (b, i, 0)),
            pl.BlockSpec((1, D_MODEL), const),
            pl.BlockSpec((D_MODEL, D_FF), const),
            pl.BlockSpec((D_FF, D_MODEL), const),
            pl.BlockSpec((1, D_MODEL), const),
        ],
        out_specs=pl.BlockSpec((1, tm, D_MODEL), lambda b, i: (b, i, 0)),
        out_shape=jax.ShapeDtypeStruct((B, T, D_MODEL), F32),
        compiler_params=_params(2),
        name="mlp",
    )(h, g, wu, wd, gf)


def _rope_tables(T):
    t = jnp.arange(T)
    inv_freq = ROPE_THETA ** (-jnp.arange(ROPE_PAIRS, dtype=F32) / ROPE_PAIRS)
    ang_r = (t // GRID_W).astype(F32)[:, None] * inv_freq[None, :]
    ang_c = (t % GRID_W).astype(F32)[:, None] * inv_freq[None, :]
    zero = jnp.zeros_like(ang_r)
    cos_h = jnp.concatenate([jnp.cos(ang_r), jnp.cos(ang_r), jnp.cos(ang_c), jnp.cos(ang_c)], -1)
    sup_h = jnp.concatenate([-jnp.sin(ang_r), zero, -jnp.sin(ang_c), zero], -1)
    sdn_h = jnp.concatenate([zero, jnp.sin(ang_r), zero, jnp.sin(ang_c)], -1)
    two = LANES // HEAD_DIM
    return jnp.tile(cos_h, (1, two)), jnp.tile(sup_h, (1, two)), jnp.tile(sdn_h, (1, two))


def _tiles(T):
    tm = min(512, T)
    tq = min(512, T)
    tk = min(1024, T)
    return tm, tq, tk


def _trunk(x, mem, p):
    T = x.shape[1]
    tm, tq, tk = _tiles(T)
    cos, sup, sdn = _rope_tables(T)
    kt, va = _mem_kv(mem, p["mem_norm"], p["w_mem_kv"])
    q, k, v, u = _in_proj(x, p["attn_norm"], p["w_a"], p["gq"], p["gk"], p["bd"], cos, sup, sdn, tm)
    attn = _flash(q, k, v, tq, tk)
    h = _merge(x, attn, u, kt, va, p["attn_norm"], p["w_b"], p["w_attn_o"], p["w_conv_o"],
               p["w_mem_o"], p["w_out"], p["conv_w"], p["conv_b"], p["conv_ln_g"], p["conv_ln_b"],
               tm, 4)
    return _mlp(h, p["mlp_norm"], p["w_up"], p["w_down"], p["final_norm"], tm)


def kernel(x_prompt, x_sample, mem_prompt, mem_sample, attn_norm, w_in, q_norm, k_norm, w_attn_o,
           conv_w, conv_b, conv_ln_g, conv_ln_b, w_conv_o, mem_norm, w_mem_kv, w_mem_o, w_out,
           mlp_norm, w_up, w_down, final_norm):
    assert w_in.shape == (1, D_MODEL, IN_COLS)
    row = lambda a: a.reshape(1, -1).astype(F32)
    two = LANES // HEAD_DIM
    head_id = jnp.arange(MXU_WIDTH) // HEAD_DIM
    p = {
        "attn_norm": row(attn_norm[0]),
        "w_a": w_in[0, :, :OFF_XQ].astype(BF16),
        "w_b": w_in[0, :, OFF_XQ:].astype(BF16),
        "gq": row(jnp.tile(q_norm[0], two)),
        "gk": row(jnp.tile(k_norm[0], two)),
        "bd": (head_id[:, None] == head_id[None, :]).astype(BF16),
        "w_attn_o": w_attn_o[0].astype(BF16),
        "conv_w": conv_w[0].astype(F32),
        "conv_b": row(conv_b[0]),
        "conv_ln_g": row(conv_ln_g[0]),
        "conv_ln_b": row(conv_ln_b[0]),
        "w_conv_o": w_conv_o[0].astype(BF16),
        "mem_norm": row(mem_norm[0]),
        "w_mem_kv": w_mem_kv[0].astype(BF16),
        "w_mem_o": w_mem_o[0].astype(BF16),
        "w_out": w_out[0].astype(BF16),
        "mlp_norm": row(mlp_norm[0]),
        "w_up": w_up[0].astype(BF16),
        "w_down": w_down[0].astype(BF16),
        "final_norm": row(final_norm),
    }
    return (_trunk(x_prompt, mem_prompt, p), _trunk(x_sample, mem_sample, p))
```

```python
import functools

import jax
import jax.numpy as jnp
from jax import lax
from jax.experimental import pallas as pl
from jax.experimental.pallas import tpu as pltpu

F32 = jnp.float32
BF16 = jnp.bfloat16

D_MODEL = 1024
GRID_W = 64
N_HEADS = 8
N_KV_HEADS = 2
HEAD_DIM = 64
GQA_GROUP = N_HEADS // N_KV_HEADS
ATTN_W = N_HEADS * HEAD_DIM
KV_W = N_KV_HEADS * HEAD_DIM
GROUP_W = GQA_GROUP * HEAD_DIM
ROPE_THETA = 10000.0
ROPE_PAIRS = HEAD_DIM // 4
CONV_CH = 512
CONV_WIDTH = 31
CONV_PAD = CONV_WIDTH // 2
N_MEM = 256
MEM_HEADS = 4
MEM_HEAD_DIM = 128
MEM_W = MEM_HEADS * MEM_HEAD_DIM
N_BRANCH = 3
D_FF = 4 * D_MODEL
RMS_EPS = 1e-6
LN_EPS = 1e-5
OFF_Q = 0
OFF_K = OFF_Q + ATTN_W
OFF_V = OFF_K + KV_W
OFF_GLU = OFF_V + KV_W
OFF_XQ = OFF_GLU + 2 * CONV_CH
OFF_GATE = OFF_XQ + MEM_W
IN_COLS = OFF_GATE + N_BRANCH * D_MODEL

LANES = 128
SUBLANES = 8
MXU_WIDTH = 256
PV_ROWS = HEAD_DIM + 16
CONV_HALO = 16
NEG_BIG = -1e30
LOG2_E = 1.4426950408889634
VMEM_LIMIT_BYTES = 56 * 1024 * 1024


def _params(n_grid_dims, flags=None):
    return pltpu.CompilerParams(
        dimension_semantics=("arbitrary",) * n_grid_dims,
        vmem_limit_bytes=VMEM_LIMIT_BYTES,
        flags=flags)


def _rms_norm(x, gain):
    ms = jnp.mean(x * x, axis=-1, keepdims=True)
    return x * lax.rsqrt(ms + RMS_EPS) * gain


def _mem_kv_kernel(mem_ref, g_ref, w_ref, kt_ref, va_ref):
    mn = _rms_norm(mem_ref[0], g_ref[...]).astype(BF16)
    kv = jnp.dot(mn, w_ref[...], preferred_element_type=F32)
    kt_ref[0] = kv[:, :MEM_W].T.astype(BF16)
    ones = jnp.ones((N_MEM, MEM_HEAD_DIM), BF16)
    for h in range(MEM_HEADS):
        lo = MEM_W + h * MEM_HEAD_DIM
        va_ref[0, h, :, :MEM_HEAD_DIM] = kv[:, lo:lo + MEM_HEAD_DIM].astype(BF16)
        va_ref[0, h, :, MEM_HEAD_DIM:] = ones


def _mem_kv(mem, g, w):
    B = mem.shape[0]
    return pl.pallas_call(
        _mem_kv_kernel,
        grid=(B,),
        in_specs=[
            pl.BlockSpec((1, N_MEM, D_MODEL), lambda b: (b, 0, 0)),
            pl.BlockSpec((1, D_MODEL), lambda b: (0, 0)),
            pl.BlockSpec((D_MODEL, 2 * MEM_W), lambda b: (0, 0)),
        ],
        out_specs=[
            pl.BlockSpec((1, MEM_W, N_MEM), lambda b: (b, 0, 0)),
            pl.BlockSpec((1, MEM_HEADS, N_MEM, 2 * MEM_HEAD_DIM), lambda b: (b, 0, 0, 0)),
        ],
        out_shape=[
            jax.ShapeDtypeStruct((B, MEM_W, N_MEM), BF16),
            jax.ShapeDtypeStruct((B, MEM_HEADS, N_MEM, 2 * MEM_HEAD_DIM), BF16),
        ],
        compiler_params=_params(1),
        name="mem_kv",
    )(mem, g, w)


def _head_norm_rope(z, gain, bd, cos, sin_up, sin_dn, scale):
    width = z.shape[1]
    rep = width // LANES
    wide = lambda t: t if rep == 1 else jnp.concatenate([t] * rep, axis=1)
    ssq = jnp.dot((z * z).astype(BF16), bd[:width, :width], preferred_element_type=F32)
    y = z * lax.rsqrt(ssq * (1.0 / HEAD_DIM) + RMS_EPS) * wide(gain)
    up = pltpu.roll(y, width - ROPE_PAIRS, 1)
    dn = pltpu.roll(y, ROPE_PAIRS, 1)
    return (y * wide(cos) + up * wide(sin_up) + dn * wide(sin_dn)) * scale


def _in_proj_kernel(x_ref, g_ref, w_ref, gq_ref, gk_ref, bd_ref, cos_ref, sup_ref, sdn_ref,
                    q_ref, k_ref, v_ref, u_ref):
    xn = _rms_norm(x_ref[0], g_ref[...]).astype(BF16)
    proj = lambda lo, hi: jnp.dot(xn, w_ref[:, lo:hi], preferred_element_type=F32)
    bd = bd_ref[...]
    cos, sup, sdn = cos_ref[...], sup_ref[...], sdn_ref[...]
    q_scale = HEAD_DIM ** -0.5 * LOG2_E
    zq = proj(OFF_Q, OFF_K)
    zkv = proj(OFF_K, OFF_GLU)
    for c in range(ATTN_W // MXU_WIDTH):
        cols = slice(c * MXU_WIDTH, (c + 1) * MXU_WIDTH)
        q_ref[0, :, cols] = _head_norm_rope(zq[:, cols], gq_ref[...], bd, cos, sup, sdn,
                                            q_scale).astype(BF16)
    za = proj(OFF_GLU, OFF_GLU + CONV_CH)
    k_ref[0] = _head_norm_rope(zkv[:, :KV_W], gk_ref[...], bd, cos, sup, sdn, 1.0).astype(BF16)
    v_ref[0] = zkv[:, KV_W:].astype(BF16)
    zb = proj(OFF_GLU + CONV_CH, OFF_XQ)
    u_ref[0] = (za * jax.nn.sigmoid(zb)).astype(BF16)


def _in_proj(x, g, w, gq, gk, bd, cos, sup, sdn, tm):
    B, T, _ = x.shape
    const = lambda b, i: (0, 0)
    tab = pl.BlockSpec((tm, LANES), lambda b, i: (i, 0))
    return pl.pallas_call(
        _in_proj_kernel,
        grid=(B, T // tm),
        in_specs=[
            pl.BlockSpec((1, tm, D_MODEL), lambda b, i: (b, i, 0)),
            pl.BlockSpec((1, D_MODEL), const),
            pl.BlockSpec((D_MODEL, OFF_XQ), const),
            pl.BlockSpec((1, LANES), const),
            pl.BlockSpec((1, LANES), const),
            pl.BlockSpec((MXU_WIDTH, MXU_WIDTH), const),
            tab, tab, tab,
        ],
        out_specs=[
            pl.BlockSpec((1, tm, ATTN_W), lambda b, i: (b, i, 0)),
            pl.BlockSpec((1, tm, KV_W), lambda b, i: (b, i, 0)),
            pl.BlockSpec((1, tm, KV_W), lambda b, i: (b, i, 0)),
            pl.BlockSpec((1, tm, CONV_CH), lambda b, i: (b, i, 0)),
        ],
        out_shape=[
            jax.ShapeDtypeStruct((B, T, ATTN_W), BF16),
            jax.ShapeDtypeStruct((B, T, KV_W), BF16),
            jax.ShapeDtypeStruct((B, T, KV_W), BF16),
            jax.ShapeDtypeStruct((B, T, CONV_CH), BF16),
        ],
        compiler_params=_params(2),
        name="in_proj",
    )(x, g, w, gq, gk, bd, cos, sup, sdn)


def _flash_kernel(q_ref, qn_ref, k_ref, v_ref, o_ref, vt_ref, qp_ref, m_ref, acc_ref,
                  s_a, s_b, mx_a, mx_b, *, tk, tq, tc, n_kv):
    g = pl.program_id(1)
    j = pl.program_id(2)
    is_g0 = g == 0
    n_ch = tk // tc

    def make_qp(src_ref, slot):
        qt = src_ref[0].astype(F32).T
        zeros = jnp.zeros((HEAD_DIM, tq), F32)
        for i in range(GQA_GROUP):
            qi = qt[i * HEAD_DIM:(i + 1) * HEAD_DIM]
            qp_ref[slot, i, :HEAD_DIM, :] = jnp.where(is_g0, qi, zeros).astype(BF16)
            qp_ref[slot, i, HEAD_DIM:, :] = jnp.where(is_g0, zeros, qi).astype(BF16)

    def qk_head(n, slot, i, s_out, mx_out):
        base = pl.multiple_of(n * tk, tk)
        mi = None
        for c in range(n_ch):
            kc = k_ref[0, pl.ds(base + c * tc, tc), :]
            s = jnp.dot(kc, qp_ref[slot, i], preferred_element_type=F32)
            s_out[i, c * tc:(c + 1) * tc, :] = s
            mc = jnp.max(s, axis=0, keepdims=True)
            mi = mc if mi is None else jnp.maximum(mi, mc)
        mx_out[i] = mi

    def fused_head(n_next, slot, vtb, i, s_in, mx_in, s_out, mx_out):
        base = pl.multiple_of(n_next * tk, tk)
        m_old = m_ref[i]
        m_new = jnp.maximum(m_old, mx_in[i])
        mi = None
        pv = None
        for c in range(n_ch):
            rows = slice(c * tc, (c + 1) * tc)
            kc = k_ref[0, pl.ds(base + c * tc, tc), :]
            s = jnp.dot(kc, qp_ref[slot, i], preferred_element_type=F32)
            s_out[i, rows, :] = s
            mc = jnp.max(s, axis=0, keepdims=True)
            mi = mc if mi is None else jnp.maximum(mi, mc)
            p = jnp.exp2(s_in[i, rows, :] - m_new).astype(BF16)
            d = jnp.dot(vtb[:, rows], p, preferred_element_type=F32)
            pv = d if pv is None else pv + d
        mx_out[i] = mi
        acc_ref[i] = acc_ref[i] * jnp.exp2(m_old - m_new) + pv
        m_ref[i] = m_new

    buf_a = (s_a, mx_a)
    buf_b = (s_b, mx_b)

    @pl.when(j == 0)
    def _():
        ones = jnp.ones((PV_ROWS - HEAD_DIM, tk), BF16)

        def build(c, carry):
            vc = v_ref[0, pl.ds(pl.multiple_of(c * tk, tk), tk), :].astype(F32)
            vct = vc.T
            top = jnp.where(is_g0, vct[:HEAD_DIM], vct[HEAD_DIM:])
            vt_ref[c, :HEAD_DIM, :] = top.astype(BF16)
            vt_ref[c, HEAD_DIM:, :] = ones
            return carry

        lax.fori_loop(0, n_kv, build, 0)
        make_qp(q_ref, 0)
        for i in range(GQA_GROUP):
            qk_head(0, 0, i, *buf_a)

    @pl.when(j > 0)
    def _():
        qp_ref[0] = qp_ref[1]

    make_qp(qn_ref, 1)
    m_ref[...] = jnp.full(m_ref.shape, NEG_BIG, F32)
    acc_ref[...] = jnp.zeros(acc_ref.shape, F32)

    def step(n, cur, nxt, n_next, slot):
        vtb = vt_ref[n]
        for i in range(GQA_GROUP):
            fused_head(n_next, slot, vtb, i, *cur, *nxt)

    def pair(t, wrap):
        n = 2 * t
        step(n, buf_a, buf_b, n + 1, 0)
        if wrap:
            step(n + 1, buf_b, buf_a, 0, 1)
        else:
            step(n + 1, buf_b, buf_a, n + 2, 0)

    def loop_body(t, carry):
        pair(t, False)
        return carry

    lax.fori_loop(0, n_kv // 2 - 1, loop_body, 0)
    pair(n_kv // 2 - 1, True)

    outs = []
    for i in range(GQA_GROUP):
        a = acc_ref[i]
        outs.append(a[:HEAD_DIM] / a[HEAD_DIM:HEAD_DIM + 1])
    o_ref[0] = jnp.concatenate(outs, axis=0).T.astype(BF16)


def _flash(q, k, v, tq, tk):
    B, T, _ = q.shape
    n_kv = T // tk
    n_q = T // tq
    assert n_kv % 2 == 0
    tc = min(256, tk)
    kern = functools.partial(_flash_kernel, tk=tk, tq=tq, tc=tc, n_kv=n_kv)
    s_buf = pltpu.VMEM((GQA_GROUP, tk, tq), F32)
    row_buf = pltpu.VMEM((GQA_GROUP, 1, tq), F32)
    return pl.pallas_call(
        kern,
        grid=(B, N_KV_HEADS, n_q),
        in_specs=[
            pl.BlockSpec((1, tq, GROUP_W), lambda b, g, j: (b, j, g)),
            pl.BlockSpec((1, tq, GROUP_W), lambda b, g, j: (b, jnp.minimum(j + 1, n_q - 1), g)),
            pl.BlockSpec((1, T, KV_W), lambda b, g, j: (b, 0, 0)),
            pl.BlockSpec((1, T, KV_W), lambda b, g, j: (b, 0, 0)),
        ],
        out_specs=pl.BlockSpec((1, tq, GROUP_W), lambda b, g, j: (b, j, g)),
        out_shape=jax.ShapeDtypeStruct((B, T, ATTN_W), BF16),
        scratch_shapes=[
            pltpu.VMEM((n_kv, PV_ROWS, tk), BF16),
            pltpu.VMEM((2, GQA_GROUP, 2 * HEAD_DIM, tq), BF16),
            row_buf,
            pltpu.VMEM((GQA_GROUP, PV_ROWS, tq), F32),
            s_buf, s_buf, row_buf, row_buf,
        ],
        compiler_params=_params(3),
        name="flash",
    )(q, q, k, v)


def _conv_fill(i, last, prev_ref, cur_ref, next_ref, sel_ref, tail_ref, a_ref, tm):
    n_hi = 2 * CONV_HALO * SUBLANES
    halo = jnp.zeros((CONV_HALO, CONV_CH), BF16)
    head = jnp.concatenate([jnp.where(i > 0, prev_ref[0], halo), cur_ref[0, :tm - CONV_HALO, :]],
                           axis=0)
    tail = jnp.concatenate([cur_ref[0, tm - CONV_HALO:, :], jnp.where(i < last, next_ref[0], halo)],
                           axis=0)
    a_lo = jnp.dot(sel_ref[...], head, preferred_element_type=F32)
    a_ref[:tm, :] = a_lo
    up = pltpu.roll(a_lo[:n_hi], n_hi - 1, 0)
    a_tail = jnp.dot(tail_ref[...], tail, preferred_element_type=F32)
    sub = lax.broadcasted_iota(jnp.int32, (n_hi, CONV_CH), 0) % SUBLANES
    a_ref[tm:, :] = jnp.where(sub == SUBLANES - 1, a_tail, up)


def _conv_taps(a_ref, pre_ref, w_ref, b_ref, p0, slabs):
    first = CONV_HALO - CONV_PAD
    n_lt = CONV_CH // LANES
    acc = [[b_ref[:, j * LANES:(j + 1) * LANES] + jnp.zeros((SUBLANES, LANES), F32)
            for j in range(n_lt)] for _ in range(slabs)]
    for tap in range(CONV_WIDTH):
        for j in range(n_lt):
            lanes = slice(j * LANES, (j + 1) * LANES)
            wt = w_ref[tap * SUBLANES:(tap + 1) * SUBLANES, lanes]
            for sl in range(slabs):
                start = (p0 + sl + tap + first) * SUBLANES
                acc[sl][j] = acc[sl][j] + a_ref[start:start + SUBLANES, lanes] * wt
    for sl in range(slabs):
        start = (p0 + sl) * SUBLANES
        pre_ref[start:start + SUBLANES, :] = jnp.concatenate(acc[sl], axis=1)


def _conv_norm(pre_ref, unsel_ref, lg_ref, lb_ref):
    acc = pre_ref[...]
    mu = jnp.mean(acc, axis=-1, keepdims=True)
    xc = acc - mu
    var = jnp.mean(xc * xc, axis=-1, keepdims=True)
    y = xc * lax.rsqrt(var + LN_EPS) * lg_ref[...] + lb_ref[...]
    y = (y * jax.nn.sigmoid(y)).astype(BF16)
    return jnp.dot(unsel_ref[...], y, preferred_element_type=F32).astype(BF16)


def _conv_perms(tm):
    seg = tm // SUBLANES
    r = jnp.arange(tm)
    src = (r % SUBLANES) * seg + r // SUBLANES
    sel = (src[:, None] == jnp.arange(tm)[None, :]).astype(BF16)
    n_hi = 2 * CONV_HALO * SUBLANES
    rh = jnp.arange(n_hi)
    tail = ((rh % SUBLANES == SUBLANES - 1)[:, None]
            & (rh[:, None] // SUBLANES == jnp.arange(2 * CONV_HALO)[None, :])).astype(BF16)
    return sel, tail, sel.T


def _merge_kernel(x_ref, attn_ref, up_ref, uc_ref, un_ref, kt_ref, va_ref, g_ref, wb_ref, wa_ref,
                  wc_ref, wm_ref, wo_ref, sel_ref, tail_ref, unsel_ref, cw_ref, cb_ref, lg_ref, lb_ref,
                  h_ref, a_ref, pre_ref, *, tm, slabs):
    i = pl.program_id(1)
    last = pl.num_programs(1) - 1
    x = x_ref[0]
    xn = _rms_norm(x, g_ref[...]).astype(BF16)
    xq = jnp.dot(xn, wb_ref[:, :MEM_W], preferred_element_type=F32).astype(BF16)
    mem_scale = MEM_HEAD_DIM ** -0.5
    scores = []
    for h in range(MEM_HEADS):
        lo = h * MEM_HEAD_DIM
        scores.append(jnp.dot(xq[:, lo:lo + MEM_HEAD_DIM], kt_ref[0, lo:lo + MEM_HEAD_DIM, :],
                              preferred_element_type=F32) * mem_scale)
    _conv_fill(i, last, up_ref, uc_ref, un_ref, sel_ref, tail_ref, a_ref, tm)
    attn_out = jnp.dot(attn_ref[0], wa_ref[...], preferred_element_type=F32)
    zg = jnp.dot(xn, wb_ref[:, MEM_W:], preferred_element_type=F32)
    heads = []
    for h in range(MEM_HEADS):
        s = scores[h]
        p = jnp.exp(s - jnp.max(s, axis=-1, keepdims=True)).astype(BF16)
        oa = jnp.dot(p, va_ref[0, h], preferred_element_type=F32)
        heads.append(oa[:, :MEM_HEAD_DIM] / oa[:, MEM_HEAD_DIM:])
    mo = jnp.concatenate(heads, axis=-1).astype(BF16)
    mem_out = jnp.dot(mo, wm_ref[...], preferred_element_type=F32)
    for p0 in range(0, tm // SUBLANES, slabs):
        _conv_taps(a_ref, pre_ref, cw_ref, cb_ref, p0, slabs)
    c = _conv_norm(pre_ref, unsel_ref, lg_ref, lb_ref)
    conv_out = jnp.dot(c, wc_ref[...], preferred_element_type=F32)
    g0 = jax.nn.sigmoid(zg[:, :D_MODEL])
    g1 = jax.nn.sigmoid(zg[:, D_MODEL:2 * D_MODEL])
    g2 = jax.nn.sigmoid(zg[:, 2 * D_MODEL:])
    merged = (g0 * attn_out + g1 * conv_out + g2 * mem_out).astype(BF16)
    h_ref[0] = x + jnp.dot(merged, wo_ref[...], preferred_element_type=F32)


def _merge(x, attn, u, kt, va, g, wb, wa, wc, wm, wo, cw, cb, lg, lb, tm, slabs):
    B, T, _ = x.shape
    hb = tm // CONV_HALO
    n_halo_blocks = T // CONV_HALO
    seg = tm // SUBLANES
    assert seg % slabs == 0 and seg >= 2 * CONV_HALO
    sel, tail, unsel = _conv_perms(tm)
    const = lambda b, i: (0, 0)
    tok = lambda w: pl.BlockSpec((1, tm, w), lambda b, i: (b, i, 0))
    row = lambda w: pl.BlockSpec((1, w), const)
    kern = functools.partial(_merge_kernel, tm=tm, slabs=slabs)
    return pl.pallas_call(
        kern,
        grid=(B, T // tm),
        in_specs=[
            tok(D_MODEL), tok(ATTN_W),
            pl.BlockSpec((1, CONV_HALO, CONV_CH),
                         lambda b, i: (b, jnp.maximum(i * hb - 1, 0), 0)),
            tok(CONV_CH),
            pl.BlockSpec((1, CONV_HALO, CONV_CH),
                         lambda b, i: (b, jnp.minimum((i + 1) * hb, n_halo_blocks - 1), 0)),
            pl.BlockSpec((1, MEM_W, N_MEM), lambda b, i: (b, 0, 0)),
            pl.BlockSpec((1, MEM_HEADS, N_MEM, 2 * MEM_HEAD_DIM), lambda b, i: (b, 0, 0, 0)),
            row(D_MODEL),
            pl.BlockSpec((D_MODEL, MEM_W + N_BRANCH * D_MODEL), const),
            pl.BlockSpec((ATTN_W, D_MODEL), const),
            pl.BlockSpec((CONV_CH, D_MODEL), const),
            pl.BlockSpec((MEM_W, D_MODEL), const),
            pl.BlockSpec((D_MODEL, D_MODEL), const),
            pl.BlockSpec(sel.shape, const),
            pl.BlockSpec(tail.shape, const),
            pl.BlockSpec(unsel.shape, const),
            pl.BlockSpec((CONV_WIDTH * SUBLANES, CONV_CH), const),
            row(CONV_CH), row(CONV_CH), row(CONV_CH),
        ],
        out_specs=tok(D_MODEL),
        out_shape=jax.ShapeDtypeStruct((B, T, D_MODEL), F32),
        scratch_shapes=[pltpu.VMEM(((seg + 2 * CONV_HALO) * SUBLANES, CONV_CH), F32),
                        pltpu.VMEM((tm, CONV_CH), F32)],
        compiler_params=_params(2),
        name="merge",
    )(x, attn, u, u, u, kt, va, g, wb, wa, wc, wm, wo, sel, tail, unsel,
      jnp.repeat(cw, SUBLANES, axis=0), cb, lg, lb)


def _mlp_kernel(h_ref, g_ref, wu_ref, wd_ref, gf_ref, y_ref):
    h = h_ref[0]
    hn = _rms_norm(h, g_ref[...]).astype(BF16)
    a = jnp.dot(hn, wu_ref[...], preferred_element_type=F32)
    a = jnp.square(jnp.maximum(a, 0.0)).astype(BF16)
    y = h + jnp.dot(a, wd_ref[...], preferred_element_type=F32)
    y_ref[0] = _rms_norm(y, gf_ref[...])


def _mlp(h, g, wu, wd, gf, tm):
    B, T, _ = h.shape
    const = lambda b, i: (0, 0)
    return pl.pallas_call(
        _mlp_kernel,
        grid=(B, T // tm),
        in_specs=[
            pl.BlockSpec((1, tm, D_MODEL), lambda b, i: (b, i, 0)),
            pl.BlockSpec((1, D_MODEL), const),
            pl.BlockSpec((D_MODEL, D_FF), const),
            pl.BlockSpec((D_FF, D_MODEL), const),
            pl.BlockSpec((1, D_MODEL), const),
        ],
        out_specs=pl.BlockSpec((1, tm, D_MODEL), lambda b, i: (b, i, 0)),
        out_shape=jax.ShapeDtypeStruct((B, T, D_MODEL), F32),
        compiler_params=_params(2),
        name="mlp",
    )(h, g, wu, wd, gf)


def _rope_tables(T):
    t = jnp.arange(T)
    inv_freq = ROPE_THETA ** (-jnp.arange(ROPE_PAIRS, dtype=F32) / ROPE_PAIRS)
    ang_r = (t // GRID_W).astype(F32)[:, None] * inv_freq[None, :]
    ang_c = (t % GRID_W).astype(F32)[:, None] * inv_freq[None, :]
    zero = jnp.zeros_like(ang_r)
    cos_h = jnp.concatenate([jnp.cos(ang_r), jnp.cos(ang_r), jnp.cos(ang_c), jnp.cos(ang_c)], -1)
    sup_h = jnp.concatenate([-jnp.sin(ang_r), zero, -jnp.sin(ang_c), zero], -1)
    sdn_h = jnp.concatenate([zero, jnp.sin(ang_r), zero, jnp.sin(ang_c)], -1)
    two = LANES // HEAD_DIM
    return jnp.tile(cos_h, (1, two)), jnp.tile(sup_h, (1, two)), jnp.tile(sdn_h, (1, two))


def _tiles(T):
    tm = min(512, T)
    tq = min(512, T)
    tk = min(1024, T)
    return tm, tq, tk


def _trunk(x, mem, p):
    T = x.shape[1]
    tm, tq, tk = _tiles(T)
    cos, sup, sdn = _rope_tables(T)
    kt, va = _mem_kv(mem, p["mem_norm"], p["w_mem_kv"])
    q, k, v, u = _in_proj(x, p["attn_norm"], p["w_a"], p["gq"], p["gk"], p["bd"], cos, sup, sdn, 2 * tm)
    attn = _flash(q, k, v, tq, tk)
    h = _merge(x, attn, u, kt, va, p["attn_norm"], p["w_b"], p["w_attn_o"], p["w_conv_o"],
               p["w_mem_o"], p["w_out"], p["conv_w"], p["conv_b"], p["conv_ln_g"], p["conv_ln_b"],
               tm, 4)
    return _mlp(h, p["mlp_norm"], p["w_up"], p["w_down"], p["final_norm"], 2 * tm)


def kernel(x_prompt, x_sample, mem_prompt, mem_sample, attn_norm, w_in, q_norm, k_norm, w_attn_o,
           conv_w, conv_b, conv_ln_g, conv_ln_b, w_conv_o, mem_norm, w_mem_kv, w_mem_o, w_out,
           mlp_norm, w_up, w_down, final_norm):
    assert w_in.shape == (1, D_MODEL, IN_COLS)
    row = lambda a: a.reshape(1, -1).astype(F32)
    two = LANES // HEAD_DIM
    head_id = jnp.arange(MXU_WIDTH) // HEAD_DIM
    p = {
        "attn_norm": row(attn_norm[0]),
        "w_a": w_in[0, :, :OFF_XQ].astype(BF16),
        "w_b": w_in[0, :, OFF_XQ:].astype(BF16),
        "gq": row(jnp.tile(q_norm[0], two)),
        "gk": row(jnp.tile(k_norm[0], two)),
        "bd": (head_id[:, None] == head_id[None, :]).astype(BF16),
        "w_attn_o": w_attn_o[0].astype(BF16),
        "conv_w": conv_w[0].astype(F32),
        "conv_b": row(conv_b[0]),
        "conv_ln_g": row(conv_ln_g[0]),
        "conv_ln_b": row(conv_ln_b[0]),
        "w_conv_o": w_conv_o[0].astype(BF16),
        "mem_norm": row(mem_norm[0]),
        "w_mem_kv": w_mem_kv[0].astype(BF16),
        "w_mem_o": w_mem_o[0].astype(BF16),
        "w_out": w_out[0].astype(BF16),
        "mlp_norm": row(mlp_norm[0]),
        "w_up": w_up[0].astype(BF16),
        "w_down": w_down[0].astype(BF16),
        "final_norm": row(final_norm),
    }
    return (_trunk(x_prompt, mem_prompt, p), _trunk(x_sample, mem_sample, p))
```

```python
import functools

import jax
import jax.numpy as jnp
from jax import lax
from jax.experimental import pallas as pl
from jax.experimental.pallas import tpu as pltpu

F32 = jnp.float32
BF16 = jnp.bfloat16

D_MODEL = 1024
GRID_W = 64
N_HEADS = 8
N_KV_HEADS = 2
HEAD_DIM = 64
GQA_GROUP = N_HEADS // N_KV_HEADS
ATTN_W = N_HEADS * HEAD_DIM
KV_W = N_KV_HEADS * HEAD_DIM
GROUP_W = GQA_GROUP * HEAD_DIM
ROPE_THETA = 10000.0
ROPE_PAIRS = HEAD_DIM // 4
CONV_CH = 512
CONV_WIDTH = 31
CONV_PAD = CONV_WIDTH // 2
N_MEM = 256
MEM_HEADS = 4
MEM_HEAD_DIM = 128
MEM_W = MEM_HEADS * MEM_HEAD_DIM
N_BRANCH = 3
D_FF = 4 * D_MODEL
RMS_EPS = 1e-6
LN_EPS = 1e-5
OFF_Q = 0
OFF_K = OFF_Q + ATTN_W
OFF_V = OFF_K + KV_W
OFF_GLU = OFF_V + KV_W
OFF_XQ = OFF_GLU + 2 * CONV_CH
OFF_GATE = OFF_XQ + MEM_W
IN_COLS = OFF_GATE + N_BRANCH * D_MODEL

LANES = 128
SUBLANES = 8
MXU_WIDTH = 256
PV_ROWS = HEAD_DIM + 16
CONV_HALO = 16
NEG_BIG = -1e30
LOG2_E = 1.4426950408889634
SAFE_EXP2_RANGE = 40.0
BOUND_MARGIN = 1.05
VMEM_LIMIT_BYTES = 56 * 1024 * 1024


def _params(n_grid_dims, flags=None):
    return pltpu.CompilerParams(
        dimension_semantics=("arbitrary",) * n_grid_dims,
        vmem_limit_bytes=VMEM_LIMIT_BYTES,
        flags=flags)


def _rms_norm(x, gain):
    ms = jnp.mean(x * x, axis=-1, keepdims=True)
    return x * lax.rsqrt(ms + RMS_EPS) * gain


def _mem_kv_kernel(mem_ref, g_ref, w_ref, kt_ref, va_ref):
    mn = _rms_norm(mem_ref[0], g_ref[...]).astype(BF16)
    kv = jnp.dot(mn, w_ref[...], preferred_element_type=F32)
    kt_ref[0] = kv[:, :MEM_W].T.astype(BF16)
    ones = jnp.ones((N_MEM, MEM_HEAD_DIM), BF16)
    for h in range(MEM_HEADS):
        lo = MEM_W + h * MEM_HEAD_DIM
        va_ref[0, h, :, :MEM_HEAD_DIM] = kv[:, lo:lo + MEM_HEAD_DIM].astype(BF16)
        va_ref[0, h, :, MEM_HEAD_DIM:] = ones


def _mem_kv(mem, g, w):
    B = mem.shape[0]
    return pl.pallas_call(
        _mem_kv_kernel,
        grid=(B,),
        in_specs=[
            pl.BlockSpec((1, N_MEM, D_MODEL), lambda b: (b, 0, 0)),
            pl.BlockSpec((1, D_MODEL), lambda b: (0, 0)),
            pl.BlockSpec((D_MODEL, 2 * MEM_W), lambda b: (0, 0)),
        ],
        out_specs=[
            pl.BlockSpec((1, MEM_W, N_MEM), lambda b: (b, 0, 0)),
            pl.BlockSpec((1, MEM_HEADS, N_MEM, 2 * MEM_HEAD_DIM), lambda b: (b, 0, 0, 0)),
        ],
        out_shape=[
            jax.ShapeDtypeStruct((B, MEM_W, N_MEM), BF16),
            jax.ShapeDtypeStruct((B, MEM_HEADS, N_MEM, 2 * MEM_HEAD_DIM), BF16),
        ],
        compiler_params=_params(1),
        name="mem_kv",
    )(mem, g, w)


def _head_norm_rope(z, gain, bd, cos, sin_up, sin_dn, scale):
    width = z.shape[1]
    rep = width // LANES
    wide = lambda t: t if rep == 1 else jnp.concatenate([t] * rep, axis=1)
    ssq = jnp.dot((z * z).astype(BF16), bd[:width, :width], preferred_element_type=F32)
    y = z * lax.rsqrt(ssq * (1.0 / HEAD_DIM) + RMS_EPS) * wide(gain)
    up = pltpu.roll(y, width - ROPE_PAIRS, 1)
    dn = pltpu.roll(y, ROPE_PAIRS, 1)
    return (y * wide(cos) + up * wide(sin_up) + dn * wide(sin_dn)) * scale


def _in_proj_kernel(x_ref, g_ref, w_ref, gq_ref, gk_ref, bd_ref, cos_ref, sup_ref, sdn_ref,
                    q_ref, k_ref, v_ref, u_ref):
    xn = _rms_norm(x_ref[0], g_ref[...]).astype(BF16)
    proj = lambda lo, hi: jnp.dot(xn, w_ref[:, lo:hi], preferred_element_type=F32)
    bd = bd_ref[...]
    cos, sup, sdn = cos_ref[...], sup_ref[...], sdn_ref[...]
    q_scale = HEAD_DIM ** -0.5 * LOG2_E
    zq = proj(OFF_Q, OFF_K)
    zkv = proj(OFF_K, OFF_GLU)
    for c in range(ATTN_W // MXU_WIDTH):
        cols = slice(c * MXU_WIDTH, (c + 1) * MXU_WIDTH)
        q_ref[0, :, cols] = _head_norm_rope(zq[:, cols], gq_ref[...], bd, cos, sup, sdn,
                                            q_scale).astype(BF16)
    za = proj(OFF_GLU, OFF_GLU + CONV_CH)
    k_ref[0] = _head_norm_rope(zkv[:, :KV_W], gk_ref[...], bd, cos, sup, sdn, 1.0).astype(BF16)
    v_ref[0] = zkv[:, KV_W:].astype(BF16)
    zb = proj(OFF_GLU + CONV_CH, OFF_XQ)
    u_ref[0] = (za * jax.nn.sigmoid(zb)).astype(BF16)


def _in_proj(x, g, w, gq, gk, bd, cos, sup, sdn, tm):
    B, T, _ = x.shape
    const = lambda b, i: (0, 0)
    tab = pl.BlockSpec((tm, LANES), lambda b, i: (i, 0))
    return pl.pallas_call(
        _in_proj_kernel,
        grid=(B, T // tm),
        in_specs=[
            pl.BlockSpec((1, tm, D_MODEL), lambda b, i: (b, i, 0)),
            pl.BlockSpec((1, D_MODEL), const),
            pl.BlockSpec((D_MODEL, OFF_XQ), const),
            pl.BlockSpec((1, LANES), const),
            pl.BlockSpec((1, LANES), const),
            pl.BlockSpec((MXU_WIDTH, MXU_WIDTH), const),
            tab, tab, tab,
        ],
        out_specs=[
            pl.BlockSpec((1, tm, ATTN_W), lambda b, i: (b, i, 0)),
            pl.BlockSpec((1, tm, KV_W), lambda b, i: (b, i, 0)),
            pl.BlockSpec((1, tm, KV_W), lambda b, i: (b, i, 0)),
            pl.BlockSpec((1, tm, CONV_CH), lambda b, i: (b, i, 0)),
        ],
        out_shape=[
            jax.ShapeDtypeStruct((B, T, ATTN_W), BF16),
            jax.ShapeDtypeStruct((B, T, KV_W), BF16),
            jax.ShapeDtypeStruct((B, T, KV_W), BF16),
            jax.ShapeDtypeStruct((B, T, CONV_CH), BF16),
        ],
        compiler_params=_params(2),
        name="in_proj",
    )(x, g, w, gq, gk, bd, cos, sup, sdn)


def _flash_kernel(q_ref, qn_ref, k_ref, v_ref, o_ref, vt_ref, qp_ref, m_ref, acc_ref,
                  s_a, s_b, mx_a, mx_b, *, tk, tq, tc, n_kv, bounded):
    g = pl.program_id(1)
    j = pl.program_id(2)
    is_g0 = g == 0
    n_ch = tk // tc

    def make_qp(src_ref, slot):
        qt = src_ref[0].astype(F32).T
        zeros = jnp.zeros((HEAD_DIM, tq), F32)
        for i in range(GQA_GROUP):
            qi = qt[i * HEAD_DIM:(i + 1) * HEAD_DIM]
            qp_ref[slot, i, :HEAD_DIM, :] = jnp.where(is_g0, qi, zeros).astype(BF16)
            qp_ref[slot, i, HEAD_DIM:, :] = jnp.where(is_g0, zeros, qi).astype(BF16)

    def qk_head(n, slot, i, s_out, mx_out):
        base = pl.multiple_of(n * tk, tk)
        mi = None
        for c in range(n_ch):
            kc = k_ref[0, pl.ds(base + c * tc, tc), :]
            s = jnp.dot(kc, qp_ref[slot, i], preferred_element_type=F32)
            s_out[i, c * tc:(c + 1) * tc, :] = s
            if not bounded:
                mc = jnp.max(s, axis=0, keepdims=True)
                mi = mc if mi is None else jnp.maximum(mi, mc)
        if not bounded:
            mx_out[i] = mi

    def fused_head(n_next, slot, vtb, i, s_in, mx_in, s_out, mx_out):
        base = pl.multiple_of(n_next * tk, tk)
        if not bounded:
            m_old = m_ref[i]
            m_new = jnp.maximum(m_old, mx_in[i])
        mi = None
        pv = None
        for c in range(n_ch):
            rows = slice(c * tc, (c + 1) * tc)
            kc = k_ref[0, pl.ds(base + c * tc, tc), :]
            s = jnp.dot(kc, qp_ref[slot, i], preferred_element_type=F32)
            s_out[i, rows, :] = s
            if not bounded:
                mc = jnp.max(s, axis=0, keepdims=True)
                mi = mc if mi is None else jnp.maximum(mi, mc)
            e = s_in[i, rows, :] if bounded else s_in[i, rows, :] - m_new
            p = jnp.exp2(e).astype(BF16)
            d = jnp.dot(vtb[:, rows], p, preferred_element_type=F32)
            pv = d if pv is None else pv + d
        if bounded:
            acc_ref[i] = acc_ref[i] + pv
        else:
            mx_out[i] = mi
            acc_ref[i] = acc_ref[i] * jnp.exp2(m_old - m_new) + pv
            m_ref[i] = m_new

    buf_a = (s_a, mx_a)
    buf_b = (s_b, mx_b)

    @pl.when(j == 0)
    def _():
        ones = jnp.ones((PV_ROWS - HEAD_DIM, tk), BF16)

        def build(c, carry):
            vc = v_ref[0, pl.ds(pl.multiple_of(c * tk, tk), tk), :].astype(F32)
            vct = vc.T
            top = jnp.where(is_g0, vct[:HEAD_DIM], vct[HEAD_DIM:])
            vt_ref[c, :HEAD_DIM, :] = top.astype(BF16)
            vt_ref[c, HEAD_DIM:, :] = ones
            return carry

        lax.fori_loop(0, n_kv, build, 0)
        make_qp(q_ref, 0)
        for i in range(GQA_GROUP):
            qk_head(0, 0, i, *buf_a)

    @pl.when(j > 0)
    def _():
        qp_ref[0] = qp_ref[1]

    make_qp(qn_ref, 1)
    if not bounded:
        m_ref[...] = jnp.full(m_ref.shape, NEG_BIG, F32)
    acc_ref[...] = jnp.zeros(acc_ref.shape, F32)

    def step(n, cur, nxt, n_next, slot):
        vtb = vt_ref[n]
        for i in range(GQA_GROUP):
            fused_head(n_next, slot, vtb, i, *cur, *nxt)

    def pair(t, wrap):
        n = 2 * t
        step(n, buf_a, buf_b, n + 1, 0)
        if wrap:
            step(n + 1, buf_b, buf_a, 0, 1)
        else:
            step(n + 1, buf_b, buf_a, n + 2, 0)

    def loop_body(t, carry):
        pair(t, False)
        return carry

    lax.fori_loop(0, n_kv // 2 - 1, loop_body, 0)
    pair(n_kv // 2 - 1, True)

    outs = []
    for i in range(GQA_GROUP):
        a = acc_ref[i]
        outs.append(a[:HEAD_DIM] / a[HEAD_DIM:HEAD_DIM + 1])
    o_ref[0] = jnp.concatenate(outs, axis=0).T.astype(BF16)


def _flash(q, k, v, tq, tk, bounded):
    B, T, _ = q.shape
    n_kv = T // tk
    n_q = T // tq
    assert n_kv % 2 == 0
    tc = min(256, tk)
    kern = functools.partial(_flash_kernel, tk=tk, tq=tq, tc=tc, n_kv=n_kv, bounded=bounded)
    s_buf = pltpu.VMEM((GQA_GROUP, tk, tq), F32)
    row_buf = pltpu.VMEM((GQA_GROUP, 1, tq), F32)
    return pl.pallas_call(
        kern,
        grid=(B, N_KV_HEADS, n_q),
        in_specs=[
            pl.BlockSpec((1, tq, GROUP_W), lambda b, g, j: (b, j, g)),
            pl.BlockSpec((1, tq, GROUP_W), lambda b, g, j: (b, jnp.minimum(j + 1, n_q - 1), g)),
            pl.BlockSpec((1, T, KV_W), lambda b, g, j: (b, 0, 0)),
            pl.BlockSpec((1, T, KV_W), lambda b, g, j: (b, 0, 0)),
        ],
        out_specs=pl.BlockSpec((1, tq, GROUP_W), lambda b, g, j: (b, j, g)),
        out_shape=jax.ShapeDtypeStruct((B, T, ATTN_W), BF16),
        scratch_shapes=[
            pltpu.VMEM((n_kv, PV_ROWS, tk), BF16),
            pltpu.VMEM((2, GQA_GROUP, 2 * HEAD_DIM, tq), BF16),
            row_buf,
            pltpu.VMEM((GQA_GROUP, PV_ROWS, tq), F32),
            s_buf, s_buf, row_buf, row_buf,
        ],
        compiler_params=_params(3),
        name="flash",
    )(q, q, k, v)


def _conv_fill(i, last, prev_ref, cur_ref, next_ref, sel_ref, tail_ref, a_ref, tm):
    n_hi = 2 * CONV_HALO * SUBLANES
    halo = jnp.zeros((CONV_HALO, CONV_CH), BF16)
    head = jnp.concatenate([jnp.where(i > 0, prev_ref[0], halo), cur_ref[0, :tm - CONV_HALO, :]],
                           axis=0)
    tail = jnp.concatenate([cur_ref[0, tm - CONV_HALO:, :], jnp.where(i < last, next_ref[0], halo)],
                           axis=0)
    a_lo = jnp.dot(sel_ref[...], head, preferred_element_type=F32)
    a_ref[:tm, :] = a_lo
    up = pltpu.roll(a_lo[:n_hi], n_hi - 1, 0)
    a_tail = jnp.dot(tail_ref[...], tail, preferred_element_type=F32)
    sub = lax.broadcasted_iota(jnp.int32, (n_hi, CONV_CH), 0) % SUBLANES
    a_ref[tm:, :] = jnp.where(sub == SUBLANES - 1, a_tail, up)


def _conv_taps(a_ref, pre_ref, w_ref, b_ref, p0, slabs):
    first = CONV_HALO - CONV_PAD
    n_lt = CONV_CH // LANES
    acc = [[b_ref[:, j * LANES:(j + 1) * LANES] + jnp.zeros((SUBLANES, LANES), F32)
            for j in range(n_lt)] for _ in range(slabs)]
    for tap in range(CONV_WIDTH):
        for j in range(n_lt):
            lanes = slice(j * LANES, (j + 1) * LANES)
            wt = w_ref[tap * SUBLANES:(tap + 1) * SUBLANES, lanes]
            for sl in range(slabs):
                start = (p0 + sl + tap + first) * SUBLANES
                acc[sl][j] = acc[sl][j] + a_ref[start:start + SUBLANES, lanes] * wt
    for sl in range(slabs):
        start = (p0 + sl) * SUBLANES
        pre_ref[start:start + SUBLANES, :] = jnp.concatenate(acc[sl], axis=1)


def _conv_norm(pre_ref, unsel_ref, lg_ref, lb_ref):
    acc = pre_ref[...]
    mu = jnp.mean(acc, axis=-1, keepdims=True)
    xc = acc - mu
    var = jnp.mean(xc * xc, axis=-1, keepdims=True)
    y = xc * lax.rsqrt(var + LN_EPS) * lg_ref[...] + lb_ref[...]
    y = (y * jax.nn.sigmoid(y)).astype(BF16)
    return jnp.dot(unsel_ref[...], y, preferred_element_type=F32).astype(BF16)


def _conv_perms(tm):
    seg = tm // SUBLANES
    r = jnp.arange(tm)
    src = (r % SUBLANES) * seg + r // SUBLANES
    sel = (src[:, None] == jnp.arange(tm)[None, :]).astype(BF16)
    n_hi = 2 * CONV_HALO * SUBLANES
    rh = jnp.arange(n_hi)
    tail = ((rh % SUBLANES == SUBLANES - 1)[:, None]
            & (rh[:, None] // SUBLANES == jnp.arange(2 * CONV_HALO)[None, :])).astype(BF16)
    return sel, tail, sel.T


def _merge_kernel(x_ref, attn_ref, up_ref, uc_ref, un_ref, kt_ref, va_ref, g_ref, wb_ref, wa_ref,
                  wc_ref, wm_ref, wo_ref, sel_ref, tail_ref, unsel_ref, cw_ref, cb_ref, lg_ref, lb_ref,
                  h_ref, a_ref, pre_ref, *, tm, slabs):
    i = pl.program_id(1)
    last = pl.num_programs(1) - 1
    x = x_ref[0]
    xn = _rms_norm(x, g_ref[...]).astype(BF16)
    xq = jnp.dot(xn, wb_ref[:, :MEM_W], preferred_element_type=F32).astype(BF16)
    mem_scale = MEM_HEAD_DIM ** -0.5
    scores = []
    for h in range(MEM_HEADS):
        lo = h * MEM_HEAD_DIM
        scores.append(jnp.dot(xq[:, lo:lo + MEM_HEAD_DIM], kt_ref[0, lo:lo + MEM_HEAD_DIM, :],
                              preferred_element_type=F32) * mem_scale)
    _conv_fill(i, last, up_ref, uc_ref, un_ref, sel_ref, tail_ref, a_ref, tm)
    attn_out = jnp.dot(attn_ref[0], wa_ref[...], preferred_element_type=F32)
    zg = jnp.dot(xn, wb_ref[:, MEM_W:], preferred_element_type=F32)
    heads = []
    for h in range(MEM_HEADS):
        s = scores[h]
        p = jnp.exp(s - jnp.max(s, axis=-1, keepdims=True)).astype(BF16)
        oa = jnp.dot(p, va_ref[0, h], preferred_element_type=F32)
        heads.append(oa[:, :MEM_HEAD_DIM] / oa[:, MEM_HEAD_DIM:])
    mo = jnp.concatenate(heads, axis=-1).astype(BF16)
    mem_out = jnp.dot(mo, wm_ref[...], preferred_element_type=F32)
    for p0 in range(0, tm // SUBLANES, slabs):
        _conv_taps(a_ref, pre_ref, cw_ref, cb_ref, p0, slabs)
    c = _conv_norm(pre_ref, unsel_ref, lg_ref, lb_ref)
    conv_out = jnp.dot(c, wc_ref[...], preferred_element_type=F32)
    g0 = jax.nn.sigmoid(zg[:, :D_MODEL])
    g1 = jax.nn.sigmoid(zg[:, D_MODEL:2 * D_MODEL])
    g2 = jax.nn.sigmoid(zg[:, 2 * D_MODEL:])
    merged = (g0 * attn_out + g1 * conv_out + g2 * mem_out).astype(BF16)
    h_ref[0] = x + jnp.dot(merged, wo_ref[...], preferred_element_type=F32)


def _merge(x, attn, u, kt, va, g, wb, wa, wc, wm, wo, cw, cb, lg, lb, tm, slabs):
    B, T, _ = x.shape
    hb = tm // CONV_HALO
    n_halo_blocks = T // CONV_HALO
    seg = tm // SUBLANES
    assert seg % slabs == 0 and seg >= 2 * CONV_HALO
    sel, tail, unsel = _conv_perms(tm)
    const = lambda b, i: (0, 0)
    tok = lambda w: pl.BlockSpec((1, tm, w), lambda b, i: (b, i, 0))
    row = lambda w: pl.BlockSpec((1, w), const)
    kern = functools.partial(_merge_kernel, tm=tm, slabs=slabs)
    return pl.pallas_call(
        kern,
        grid=(B, T // tm),
        in_specs=[
            tok(D_MODEL), tok(ATTN_W),
            pl.BlockSpec((1, CONV_HALO, CONV_CH),
                         lambda b, i: (b, jnp.maximum(i * hb - 1, 0), 0)),
            tok(CONV_CH),
            pl.BlockSpec((1, CONV_HALO, CONV_CH),
                         lambda b, i: (b, jnp.minimum((i + 1) * hb, n_halo_blocks - 1), 0)),
            pl.BlockSpec((1, MEM_W, N_MEM), lambda b, i: (b, 0, 0)),
            pl.BlockSpec((1, MEM_HEADS, N_MEM, 2 * MEM_HEAD_DIM), lambda b, i: (b, 0, 0, 0)),
            row(D_MODEL),
            pl.BlockSpec((D_MODEL, MEM_W + N_BRANCH * D_MODEL), const),
            pl.BlockSpec((ATTN_W, D_MODEL), const),
            pl.BlockSpec((CONV_CH, D_MODEL), const),
            pl.BlockSpec((MEM_W, D_MODEL), const),
            pl.BlockSpec((D_MODEL, D_MODEL), const),
            pl.BlockSpec(sel.shape, const),
            pl.BlockSpec(tail.shape, const),
            pl.BlockSpec(unsel.shape, const),
            pl.BlockSpec((CONV_WIDTH * SUBLANES, CONV_CH), const),
            row(CONV_CH), row(CONV_CH), row(CONV_CH),
        ],
        out_specs=tok(D_MODEL),
        out_shape=jax.ShapeDtypeStruct((B, T, D_MODEL), F32),
        scratch_shapes=[pltpu.VMEM(((seg + 2 * CONV_HALO) * SUBLANES, CONV_CH), F32),
                        pltpu.VMEM((tm, CONV_CH), F32)],
        compiler_params=_params(2),
        name="merge",
    )(x, attn, u, u, u, kt, va, g, wb, wa, wc, wm, wo, sel, tail, unsel,
      jnp.repeat(cw, SUBLANES, axis=0), cb, lg, lb)


def _mlp_kernel(h_ref, g_ref, wu_ref, wd_ref, gf_ref, y_ref):
    h = h_ref[0]
    hn = _rms_norm(h, g_ref[...]).astype(BF16)
    a = jnp.dot(hn, wu_ref[...], preferred_element_type=F32)
    a = jnp.square(jnp.maximum(a, 0.0)).astype(BF16)
    y = h + jnp.dot(a, wd_ref[...], preferred_element_type=F32)
    y_ref[0] = _rms_norm(y, gf_ref[...])


def _mlp(h, g, wu, wd, gf, tm):
    B, T, _ = h.shape
    const = lambda b, i: (0, 0)
    return pl.pallas_call(
        _mlp_kernel,
        grid=(B, T // tm),
        in_specs=[
            pl.BlockSpec((1, tm, D_MODEL), lambda b, i: (b, i, 0)),
            pl.BlockSpec((1, D_MODEL), const),
            pl.BlockSpec((D_MODEL, D_FF), const),
            pl.BlockSpec((D_FF, D_MODEL), const),
            pl.BlockSpec((1, D_MODEL), const),
        ],
        out_specs=pl.BlockSpec((1, tm, D_MODEL), lambda b, i: (b, i, 0)),
        out_shape=jax.ShapeDtypeStruct((B, T, D_MODEL), F32),
        compiler_params=_params(2),
        name="mlp",
    )(h, g, wu, wd, gf)


def _rope_tables(T):
    t = jnp.arange(T)
    inv_freq = ROPE_THETA ** (-jnp.arange(ROPE_PAIRS, dtype=F32) / ROPE_PAIRS)
    ang_r = (t // GRID_W).astype(F32)[:, None] * inv_freq[None, :]
    ang_c = (t % GRID_W).astype(F32)[:, None] * inv_freq[None, :]
    zero = jnp.zeros_like(ang_r)
    cos_h = jnp.concatenate([jnp.cos(ang_r), jnp.cos(ang_r), jnp.cos(ang_c), jnp.cos(ang_c)], -1)
    sup_h = jnp.concatenate([-jnp.sin(ang_r), zero, -jnp.sin(ang_c), zero], -1)
    sdn_h = jnp.concatenate([zero, jnp.sin(ang_r), zero, jnp.sin(ang_c)], -1)
    two = LANES // HEAD_DIM
    return jnp.tile(cos_h, (1, two)), jnp.tile(sup_h, (1, two)), jnp.tile(sdn_h, (1, two))


def _score_bound(q_gain, k_gain):
    gq = jnp.max(jnp.abs(q_gain.astype(F32)))
    gk = jnp.max(jnp.abs(k_gain.astype(F32)))
    return BOUND_MARGIN * HEAD_DIM ** 0.5 * LOG2_E * gq * gk


def _tiles(T):
    tm = min(512, T)
    tq = min(512, T)
    tk = min(1024, T)
    return tm, tq, tk


def _trunk(x, mem, p):
    T = x.shape[1]
    tm, tq, tk = _tiles(T)
    cos, sup, sdn = _rope_tables(T)
    kt, va = _mem_kv(mem, p["mem_norm"], p["w_mem_kv"])
    q, k, v, u = _in_proj(x, p["attn_norm"], p["w_a"], p["gq"], p["gk"], p["bd"], cos, sup, sdn, 2 * tm)
    attn = lax.cond(p["score_bound"] <= SAFE_EXP2_RANGE,
                    functools.partial(_flash, tq=tq, tk=tk, bounded=True),
                    functools.partial(_flash, tq=tq, tk=tk, bounded=False), q, k, v)
    h = _merge(x, attn, u, kt, va, p["attn_norm"], p["w_b"], p["w_attn_o"], p["w_conv_o"],
               p["w_mem_o"], p["w_out"], p["conv_w"], p["conv_b"], p["conv_ln_g"], p["conv_ln_b"],
               tm, 4)
    return _mlp(h, p["mlp_norm"], p["w_up"], p["w_down"], p["final_norm"], 2 * tm)


def kernel(x_prompt, x_sample, mem_prompt, mem_sample, attn_norm, w_in, q_norm, k_norm, w_attn_o,
           conv_w, conv_b, conv_ln_g, conv_ln_b, w_conv_o, mem_norm, w_mem_kv, w_mem_o, w_out,
           mlp_norm, w_up, w_down, final_norm):
    assert w_in.shape == (1, D_MODEL, IN_COLS)
    row = lambda a: a.reshape(1, -1).astype(F32)
    two = LANES // HEAD_DIM
    head_id = jnp.arange(MXU_WIDTH) // HEAD_DIM
    p = {
        "attn_norm": row(attn_norm[0]),
        "w_a": w_in[0, :, :OFF_XQ].astype(BF16),
        "w_b": w_in[0, :, OFF_XQ:].astype(BF16),
        "gq": row(jnp.tile(q_norm[0], two)),
        "gk": row(jnp.tile(k_norm[0], two)),
        "bd": (head_id[:, None] == head_id[None, :]).astype(BF16),
        "w_attn_o": w_attn_o[0].astype(BF16),
        "conv_w": conv_w[0].astype(F32),
        "conv_b": row(conv_b[0]),
        "conv_ln_g": row(conv_ln_g[0]),
        "conv_ln_b": row(conv_ln_b[0]),
        "w_conv_o": w_conv_o[0].astype(BF16),
        "mem_norm": row(mem_norm[0]),
        "w_mem_kv": w_mem_kv[0].astype(BF16),
        "w_mem_o": w_mem_o[0].astype(BF16),
        "w_out": w_out[0].astype(BF16),
        "mlp_norm": row(mlp_norm[0]),
        "w_up": w_up[0].astype(BF16),
        "w_down": w_down[0].astype(BF16),
        "final_norm": row(final_norm),
        "score_bound": _score_bound(q_norm[0], k_norm[0]),
    }
    return (_trunk(x_prompt, mem_prompt, p), _trunk(x_sample, mem_sample, p))
```

```python
import functools

import jax
import jax.numpy as jnp
from jax import lax
from jax.experimental import pallas as pl
from jax.experimental.pallas import tpu as pltpu

F32 = jnp.float32
BF16 = jnp.bfloat16

D_MODEL = 1024
GRID_W = 64
N_HEADS = 8
N_KV_HEADS = 2
HEAD_DIM = 64
GQA_GROUP = N_HEADS // N_KV_HEADS
ATTN_W = N_HEADS * HEAD_DIM
KV_W = N_KV_HEADS * HEAD_DIM
GROUP_W = GQA_GROUP * HEAD_DIM
ROPE_THETA = 10000.0
ROPE_PAIRS = HEAD_DIM // 4
CONV_CH = 512
CONV_WIDTH = 31
CONV_PAD = CONV_WIDTH // 2
N_MEM = 256
MEM_HEADS = 4
MEM_HEAD_DIM = 128
MEM_W = MEM_HEADS * MEM_HEAD_DIM
N_BRANCH = 3
D_FF = 4 * D_MODEL
RMS_EPS = 1e-6
LN_EPS = 1e-5
OFF_Q = 0
OFF_K = OFF_Q + ATTN_W
OFF_V = OFF_K + KV_W
OFF_GLU = OFF_V + KV_W
OFF_XQ = OFF_GLU + 2 * CONV_CH
OFF_GATE = OFF_XQ + MEM_W
IN_COLS = OFF_GATE + N_BRANCH * D_MODEL

LANES = 128
SUBLANES = 8
MXU_WIDTH = 256
PV_ROWS = HEAD_DIM + 8
CONV_HALO = 16
NEG_BIG = -1e30
LOG2_E = 1.4426950408889634
SAFE_EXP2_RANGE = 40.0
BOUND_MARGIN = 1.05
VMEM_LIMIT_BYTES = 56 * 1024 * 1024


def _params(n_grid_dims, flags=None):
    return pltpu.CompilerParams(
        dimension_semantics=("arbitrary",) * n_grid_dims,
        vmem_limit_bytes=VMEM_LIMIT_BYTES,
        flags=flags)


def _rms_norm(x, gain):
    ms = jnp.mean(x * x, axis=-1, keepdims=True)
    return x * lax.rsqrt(ms + RMS_EPS) * gain


def _mem_kv_kernel(mem_ref, g_ref, w_ref, kt_ref, va_ref):
    mn = _rms_norm(mem_ref[0], g_ref[...]).astype(BF16)
    kv = jnp.dot(mn, w_ref[...], preferred_element_type=F32)
    kt_ref[0] = kv[:, :MEM_W].T.astype(BF16)
    ones = jnp.ones((N_MEM, MEM_HEAD_DIM), BF16)
    for h in range(MEM_HEADS):
        lo = MEM_W + h * MEM_HEAD_DIM
        va_ref[0, h, :, :MEM_HEAD_DIM] = kv[:, lo:lo + MEM_HEAD_DIM].astype(BF16)
        va_ref[0, h, :, MEM_HEAD_DIM:] = ones


def _mem_kv(mem, g, w):
    B = mem.shape[0]
    return pl.pallas_call(
        _mem_kv_kernel,
        grid=(B,),
        in_specs=[
            pl.BlockSpec((1, N_MEM, D_MODEL), lambda b: (b, 0, 0)),
            pl.BlockSpec((1, D_MODEL), lambda b: (0, 0)),
            pl.BlockSpec((D_MODEL, 2 * MEM_W), lambda b: (0, 0)),
        ],
        out_specs=[
            pl.BlockSpec((1, MEM_W, N_MEM), lambda b: (b, 0, 0)),
            pl.BlockSpec((1, MEM_HEADS, N_MEM, 2 * MEM_HEAD_DIM), lambda b: (b, 0, 0, 0)),
        ],
        out_shape=[
            jax.ShapeDtypeStruct((B, MEM_W, N_MEM), BF16),
            jax.ShapeDtypeStruct((B, MEM_HEADS, N_MEM, 2 * MEM_HEAD_DIM), BF16),
        ],
        compiler_params=_params(1),
        name="mem_kv",
    )(mem, g, w)


def _head_norm_rope(z, gain, bd, cos, sin_up, sin_dn, scale):
    width = z.shape[1]
    rep = width // LANES
    wide = lambda t: t if rep == 1 else jnp.concatenate([t] * rep, axis=1)
    ssq = jnp.dot((z * z).astype(BF16), bd[:width, :width], preferred_element_type=F32)
    y = z * lax.rsqrt(ssq * (1.0 / HEAD_DIM) + RMS_EPS) * wide(gain)
    up = pltpu.roll(y, width - ROPE_PAIRS, 1)
    dn = pltpu.roll(y, ROPE_PAIRS, 1)
    return (y * wide(cos) + up * wide(sin_up) + dn * wide(sin_dn)) * scale


def _in_proj_kernel(x_ref, g_ref, w_ref, gq_ref, gk_ref, bd_ref, cos_ref, sup_ref, sdn_ref,
                    q_ref, k_ref, v_ref, u_ref):
    xn = _rms_norm(x_ref[0], g_ref[...]).astype(BF16)
    proj = lambda lo, hi: jnp.dot(xn, w_ref[:, lo:hi], preferred_element_type=F32)
    bd = bd_ref[...]
    cos, sup, sdn = cos_ref[...], sup_ref[...], sdn_ref[...]
    q_scale = HEAD_DIM ** -0.5 * LOG2_E
    zq = proj(OFF_Q, OFF_K)
    zkv = proj(OFF_K, OFF_GLU)
    for c in range(ATTN_W // MXU_WIDTH):
        cols = slice(c * MXU_WIDTH, (c + 1) * MXU_WIDTH)
        q_ref[0, :, cols] = _head_norm_rope(zq[:, cols], gq_ref[...], bd, cos, sup, sdn,
                                            q_scale).astype(BF16)
    za = proj(OFF_GLU, OFF_GLU + CONV_CH)
    k_ref[0] = _head_norm_rope(zkv[:, :KV_W], gk_ref[...], bd, cos, sup, sdn, 1.0).astype(BF16)
    v_ref[0] = zkv[:, KV_W:].astype(BF16)
    zb = proj(OFF_GLU + CONV_CH, OFF_XQ)
    u_ref[0] = (za * jax.nn.sigmoid(zb)).astype(BF16)


def _in_proj(x, g, w, gq, gk, bd, cos, sup, sdn, tm):
    B, T, _ = x.shape
    const = lambda b, i: (0, 0)
    tab = pl.BlockSpec((tm, LANES), lambda b, i: (i, 0))
    return pl.pallas_call(
        _in_proj_kernel,
        grid=(B, T // tm),
        in_specs=[
            pl.BlockSpec((1, tm, D_MODEL), lambda b, i: (b, i, 0)),
            pl.BlockSpec((1, D_MODEL), const),
            pl.BlockSpec((D_MODEL, OFF_XQ), const),
            pl.BlockSpec((1, LANES), const),
            pl.BlockSpec((1, LANES), const),
            pl.BlockSpec((MXU_WIDTH, MXU_WIDTH), const),
            tab, tab, tab,
        ],
        out_specs=[
            pl.BlockSpec((1, tm, ATTN_W), lambda b, i: (b, i, 0)),
            pl.BlockSpec((1, tm, KV_W), lambda b, i: (b, i, 0)),
            pl.BlockSpec((1, tm, KV_W), lambda b, i: (b, i, 0)),
            pl.BlockSpec((1, tm, CONV_CH), lambda b, i: (b, i, 0)),
        ],
        out_shape=[
            jax.ShapeDtypeStruct((B, T, ATTN_W), BF16),
            jax.ShapeDtypeStruct((B, T, KV_W), BF16),
            jax.ShapeDtypeStruct((B, T, KV_W), BF16),
            jax.ShapeDtypeStruct((B, T, CONV_CH), BF16),
        ],
        compiler_params=_params(2),
        name="in_proj",
    )(x, g, w, gq, gk, bd, cos, sup, sdn)


def _flash_kernel(q_ref, qn_ref, k_ref, v_ref, o_ref, vt_ref, qp_ref, m_ref, acc_ref,
                  s_a, s_b, mx_a, mx_b, *, tk, tq, tc, n_kv, bounded):
    g = pl.program_id(1)
    j = pl.program_id(2)
    is_g0 = g == 0
    n_ch = tk // tc

    def make_qp(src_ref, slot):
        qt = src_ref[0].astype(F32).T
        zeros = jnp.zeros((HEAD_DIM, tq), F32)
        for i in range(GQA_GROUP):
            qi = qt[i * HEAD_DIM:(i + 1) * HEAD_DIM]
            qp_ref[slot, i, :HEAD_DIM, :] = jnp.where(is_g0, qi, zeros).astype(BF16)
            qp_ref[slot, i, HEAD_DIM:, :] = jnp.where(is_g0, zeros, qi).astype(BF16)

    def qk_head(n, slot, i, s_out, mx_out):
        base = pl.multiple_of(n * tk, tk)
        mi = None
        for c in range(n_ch):
            kc = k_ref[0, pl.ds(base + c * tc, tc), :]
            s = jnp.dot(kc, qp_ref[slot, i], preferred_element_type=F32)
            s_out[i, c * tc:(c + 1) * tc, :] = s
            if not bounded:
                mc = jnp.max(s, axis=0, keepdims=True)
                mi = mc if mi is None else jnp.maximum(mi, mc)
        if not bounded:
            mx_out[i] = mi

    def fused_head(n_next, slot, vtb, i, s_in, mx_in, s_out, mx_out):
        base = pl.multiple_of(n_next * tk, tk)
        if not bounded:
            m_old = m_ref[i]
            m_new = jnp.maximum(m_old, mx_in[i])
        mi = None
        pv = None
        for c in range(n_ch):
            rows = slice(c * tc, (c + 1) * tc)
            kc = k_ref[0, pl.ds(base + c * tc, tc), :]
            s = jnp.dot(kc, qp_ref[slot, i], preferred_element_type=F32)
            s_out[i, rows, :] = s
            if not bounded:
                mc = jnp.max(s, axis=0, keepdims=True)
                mi = mc if mi is None else jnp.maximum(mi, mc)
            e = s_in[i, rows, :] if bounded else s_in[i, rows, :] - m_new
            p = jnp.exp2(e)
            d = jnp.dot(vtb[:, rows], p, preferred_element_type=F32)
            pv = d if pv is None else pv + d
        if bounded:
            acc_ref[i] = acc_ref[i] + pv
        else:
            mx_out[i] = mi
            acc_ref[i] = acc_ref[i] * jnp.exp2(m_old - m_new) + pv
            m_ref[i] = m_new

    buf_a = (s_a, mx_a)
    buf_b = (s_b, mx_b)

    @pl.when(j == 0)
    def _():
        ones = jnp.ones((PV_ROWS - HEAD_DIM, tk), F32)

        def build(c, carry):
            vc = v_ref[0, pl.ds(pl.multiple_of(c * tk, tk), tk), :].astype(F32)
            vct = vc.T
            top = jnp.where(is_g0, vct[:HEAD_DIM], vct[HEAD_DIM:])
            vt_ref[c, :HEAD_DIM, :] = top
            vt_ref[c, HEAD_DIM:, :] = ones
            return carry

        lax.fori_loop(0, n_kv, build, 0)
        make_qp(q_ref, 0)
        for i in range(GQA_GROUP):
            qk_head(0, 0, i, *buf_a)

    @pl.when(j > 0)
    def _():
        qp_ref[0] = qp_ref[1]

    make_qp(qn_ref, 1)
    if not bounded:
        m_ref[...] = jnp.full(m_ref.shape, NEG_BIG, F32)
    acc_ref[...] = jnp.zeros(acc_ref.shape, F32)

    def step(n, cur, nxt, n_next, slot):
        vtb = vt_ref[n]
        for i in range(GQA_GROUP):
            fused_head(n_next, slot, vtb, i, *cur, *nxt)

    def pair(t, wrap):
        n = 2 * t
        step(n, buf_a, buf_b, n + 1, 0)
        if wrap:
            step(n + 1, buf_b, buf_a, 0, 1)
        else:
            step(n + 1, buf_b, buf_a, n + 2, 0)

    def loop_body(t, carry):
        pair(t, False)
        return carry

    lax.fori_loop(0, n_kv // 2 - 1, loop_body, 0)
    pair(n_kv // 2 - 1, True)

    outs = []
    for i in range(GQA_GROUP):
        a = acc_ref[i]
        outs.append(a[:HEAD_DIM] / a[HEAD_DIM:HEAD_DIM + 1])
    o_ref[0] = jnp.concatenate(outs, axis=0).T.astype(BF16)


def _flash(q, k, v, tq, tk, bounded):
    B, T, _ = q.shape
    n_kv = T // tk
    n_q = T // tq
    assert n_kv % 2 == 0
    tc = min(256, tk)
    kern = functools.partial(_flash_kernel, tk=tk, tq=tq, tc=tc, n_kv=n_kv, bounded=bounded)
    s_buf = pltpu.VMEM((GQA_GROUP, tk, tq), F32)
    row_buf = pltpu.VMEM((GQA_GROUP, 1, tq), F32)
    return pl.pallas_call(
        kern,
        grid=(B, N_KV_HEADS, n_q),
        in_specs=[
            pl.BlockSpec((1, tq, GROUP_W), lambda b, g, j: (b, j, g)),
            pl.BlockSpec((1, tq, GROUP_W), lambda b, g, j: (b, jnp.minimum(j + 1, n_q - 1), g)),
            pl.BlockSpec((1, T, KV_W), lambda b, g, j: (b, 0, 0)),
            pl.BlockSpec((1, T, KV_W), lambda b, g, j: (b, 0, 0)),
        ],
        out_specs=pl.BlockSpec((1, tq, GROUP_W), lambda b, g, j: (b, j, g)),
        out_shape=jax.ShapeDtypeStruct((B, T, ATTN_W), BF16),
        scratch_shapes=[
            pltpu.VMEM((n_kv, PV_ROWS, tk), F32),
            pltpu.VMEM((2, GQA_GROUP, 2 * HEAD_DIM, tq), BF16),
            row_buf,
            pltpu.VMEM((GQA_GROUP, PV_ROWS, tq), F32),
            s_buf, s_buf, row_buf, row_buf,
        ],
        compiler_params=_params(3),
        name="flash",
    )(q, q, k, v)


def _conv_fill(i, last, prev_ref, cur_ref, next_ref, sel_ref, tail_ref, a_ref, tm):
    n_hi = 2 * CONV_HALO * SUBLANES
    halo = jnp.zeros((CONV_HALO, CONV_CH), BF16)
    head = jnp.concatenate([jnp.where(i > 0, prev_ref[0], halo), cur_ref[0, :tm - CONV_HALO, :]],
                           axis=0)
    tail = jnp.concatenate([cur_ref[0, tm - CONV_HALO:, :], jnp.where(i < last, next_ref[0], halo)],
                           axis=0)
    a_lo = jnp.dot(sel_ref[...], head, preferred_element_type=F32)
    a_ref[:tm, :] = a_lo
    up = pltpu.roll(a_lo[:n_hi], n_hi - 1, 0)
    a_tail = jnp.dot(tail_ref[...], tail, preferred_element_type=F32)
    sub = lax.broadcasted_iota(jnp.int32, (n_hi, CONV_CH), 0) % SUBLANES
    a_ref[tm:, :] = jnp.where(sub == SUBLANES - 1, a_tail, up)


def _conv_taps(a_ref, pre_ref, w_ref, b_ref, p0, slabs):
    first = CONV_HALO - CONV_PAD
    n_lt = CONV_CH // LANES
    acc = [[b_ref[:, j * LANES:(j + 1) * LANES] + jnp.zeros((SUBLANES, LANES), F32)
            for j in range(n_lt)] for _ in range(slabs)]
    for tap in range(CONV_WIDTH):
        for j in range(n_lt):
            lanes = slice(j * LANES, (j + 1) * LANES)
            wt = w_ref[tap * SUBLANES:(tap + 1) * SUBLANES, lanes]
            for sl in range(slabs):
                start = (p0 + sl + tap + first) * SUBLANES
                acc[sl][j] = acc[sl][j] + a_ref[start:start + SUBLANES, lanes] * wt
    for sl in range(slabs):
        start = (p0 + sl) * SUBLANES
        pre_ref[start:start + SUBLANES, :] = jnp.concatenate(acc[sl], axis=1)


def _conv_norm(pre_ref, unsel_ref, lg_ref, lb_ref):
    acc = pre_ref[...]
    mu = jnp.mean(acc, axis=-1, keepdims=True)
    xc = acc - mu
    var = jnp.mean(xc * xc, axis=-1, keepdims=True)
    y = xc * lax.rsqrt(var + LN_EPS) * lg_ref[...] + lb_ref[...]
    y = (y * jax.nn.sigmoid(y)).astype(BF16)
    return jnp.dot(unsel_ref[...], y, preferred_element_type=F32).astype(BF16)


def _conv_perms(tm):
    seg = tm // SUBLANES
    r = jnp.arange(tm)
    src = (r % SUBLANES) * seg + r // SUBLANES
    sel = (src[:, None] == jnp.arange(tm)[None, :]).astype(BF16)
    n_hi = 2 * CONV_HALO * SUBLANES
    rh = jnp.arange(n_hi)
    tail = ((rh % SUBLANES == SUBLANES - 1)[:, None]
            & (rh[:, None] // SUBLANES == jnp.arange(2 * CONV_HALO)[None, :])).astype(BF16)
    return sel, tail, sel.T


def _merge_kernel(x_ref, attn_ref, up_ref, uc_ref, un_ref, kt_ref, va_ref, g_ref, wb_ref, wa_ref,
                  wc_ref, wm_ref, wo_ref, sel_ref, tail_ref, unsel_ref, cw_ref, cb_ref, lg_ref, lb_ref,
                  h_ref, a_ref, pre_ref, *, tm, slabs):
    i = pl.program_id(1)
    last = pl.num_programs(1) - 1
    x = x_ref[0]
    xn = _rms_norm(x, g_ref[...]).astype(BF16)
    xq = jnp.dot(xn, wb_ref[:, :MEM_W], preferred_element_type=F32).astype(BF16)
    mem_scale = MEM_HEAD_DIM ** -0.5
    scores = []
    for h in range(MEM_HEADS):
        lo = h * MEM_HEAD_DIM
        scores.append(jnp.dot(xq[:, lo:lo + MEM_HEAD_DIM], kt_ref[0, lo:lo + MEM_HEAD_DIM, :],
                              preferred_element_type=F32) * mem_scale)
    _conv_fill(i, last, up_ref, uc_ref, un_ref, sel_ref, tail_ref, a_ref, tm)
    attn_out = jnp.dot(attn_ref[0], wa_ref[...], preferred_element_type=F32)
    zg = jnp.dot(xn, wb_ref[:, MEM_W:], preferred_element_type=F32)
    heads = []
    for h in range(MEM_HEADS):
        s = scores[h]
        p = jnp.exp(s - jnp.max(s, axis=-1, keepdims=True)).astype(BF16)
        oa = jnp.dot(p, va_ref[0, h], preferred_element_type=F32)
        heads.append(oa[:, :MEM_HEAD_DIM] / oa[:, MEM_HEAD_DIM:])
    mo = jnp.concatenate(heads, axis=-1).astype(BF16)
    mem_out = jnp.dot(mo, wm_ref[...], preferred_element_type=F32)
    for p0 in range(0, tm // SUBLANES, slabs):
        _conv_taps(a_ref, pre_ref, cw_ref, cb_ref, p0, slabs)
    c = _conv_norm(pre_ref, unsel_ref, lg_ref, lb_ref)
    conv_out = jnp.dot(c, wc_ref[...], preferred_element_type=F32)
    g0 = jax.nn.sigmoid(zg[:, :D_MODEL])
    g1 = jax.nn.sigmoid(zg[:, D_MODEL:2 * D_MODEL])
    g2 = jax.nn.sigmoid(zg[:, 2 * D_MODEL:])
    merged = (g0 * attn_out + g1 * conv_out + g2 * mem_out).astype(BF16)
    h_ref[0] = x + jnp.dot(merged, wo_ref[...], preferred_element_type=F32)


def _merge(x, attn, u, kt, va, g, wb, wa, wc, wm, wo, cw, cb, lg, lb, tm, slabs):
    B, T, _ = x.shape
    hb = tm // CONV_HALO
    n_halo_blocks = T // CONV_HALO
    seg = tm // SUBLANES
    assert seg % slabs == 0 and seg >= 2 * CONV_HALO
    sel, tail, unsel = _conv_perms(tm)
    const = lambda b, i: (0, 0)
    tok = lambda w: pl.BlockSpec((1, tm, w), lambda b, i: (b, i, 0))
    row = lambda w: pl.BlockSpec((1, w), const)
    kern = functools.partial(_merge_kernel, tm=tm, slabs=slabs)
    return pl.pallas_call(
        kern,
        grid=(B, T // tm),
        in_specs=[
            tok(D_MODEL), tok(ATTN_W),
            pl.BlockSpec((1, CONV_HALO, CONV_CH),
                         lambda b, i: (b, jnp.maximum(i * hb - 1, 0), 0)),
            tok(CONV_CH),
            pl.BlockSpec((1, CONV_HALO, CONV_CH),
                         lambda b, i: (b, jnp.minimum((i + 1) * hb, n_halo_blocks - 1), 0)),
            pl.BlockSpec((1, MEM_W, N_MEM), lambda b, i: (b, 0, 0)),
            pl.BlockSpec((1, MEM_HEADS, N_MEM, 2 * MEM_HEAD_DIM), lambda b, i: (b, 0, 0, 0)),
            row(D_MODEL),
            pl.BlockSpec((D_MODEL, MEM_W + N_BRANCH * D_MODEL), const),
            pl.BlockSpec((ATTN_W, D_MODEL), const),
            pl.BlockSpec((CONV_CH, D_MODEL), const),
            pl.BlockSpec((MEM_W, D_MODEL), const),
            pl.BlockSpec((D_MODEL, D_MODEL), const),
            pl.BlockSpec(sel.shape, const),
            pl.BlockSpec(tail.shape, const),
            pl.BlockSpec(unsel.shape, const),
            pl.BlockSpec((CONV_WIDTH * SUBLANES, CONV_CH), const),
            row(CONV_CH), row(CONV_CH), row(CONV_CH),
        ],
        out_specs=tok(D_MODEL),
        out_shape=jax.ShapeDtypeStruct((B, T, D_MODEL), F32),
        scratch_shapes=[pltpu.VMEM(((seg + 2 * CONV_HALO) * SUBLANES, CONV_CH), F32),
                        pltpu.VMEM((tm, CONV_CH), F32)],
        compiler_params=_params(2),
        name="merge",
    )(x, attn, u, u, u, kt, va, g, wb, wa, wc, wm, wo, sel, tail, unsel,
      jnp.repeat(cw, SUBLANES, axis=0), cb, lg, lb)


def _mlp_kernel(h_ref, g_ref, wu_ref, wd_ref, gf_ref, y_ref):
    h = h_ref[0]
    hn = _rms_norm(h, g_ref[...]).astype(BF16)
    a = jnp.dot(hn, wu_ref[...], preferred_element_type=F32)
    a = jnp.square(jnp.maximum(a, 0.0)).astype(BF16)
    y = h + jnp.dot(a, wd_ref[...], preferred_element_type=F32)
    y_ref[0] = _rms_norm(y, gf_ref[...])


def _mlp(h, g, wu, wd, gf, tm):
    B, T, _ = h.shape
    const = lambda b, i: (0, 0)
    return pl.pallas_call(
        _mlp_kernel,
        grid=(B, T // tm),
        in_specs=[
            pl.BlockSpec((1, tm, D_MODEL), lambda b, i: (b, i, 0)),
            pl.BlockSpec((1, D_MODEL), const),
            pl.BlockSpec((D_MODEL, D_FF), const),
            pl.BlockSpec((D_FF, D_MODEL), const),
            pl.BlockSpec((1, D_MODEL), const),
        ],
        out_specs=pl.BlockSpec((1, tm, D_MODEL), lambda b, i: (b, i, 0)),
        out_shape=jax.ShapeDtypeStruct((B, T, D_MODEL), F32),
        compiler_params=_params(2),
        name="mlp",
    )(h, g, wu, wd, gf)


def _rope_tables(T):
    t = jnp.arange(T)
    inv_freq = ROPE_THETA ** (-jnp.arange(ROPE_PAIRS, dtype=F32) / ROPE_PAIRS)
    ang_r = (t // GRID_W).astype(F32)[:, None] * inv_freq[None, :]
    ang_c = (t % GRID_W).astype(F32)[:, None] * inv_freq[None, :]
    zero = jnp.zeros_like(ang_r)
    cos_h = jnp.concatenate([jnp.cos(ang_r), jnp.cos(ang_r), jnp.cos(ang_c), jnp.cos(ang_c)], -1)
    sup_h = jnp.concatenate([-jnp.sin(ang_r), zero, -jnp.sin(ang_c), zero], -1)
    sdn_h = jnp.concatenate([zero, jnp.sin(ang_r), zero, jnp.sin(ang_c)], -1)
    two = LANES // HEAD_DIM
    return jnp.tile(cos_h, (1, two)), jnp.tile(sup_h, (1, two)), jnp.tile(sdn_h, (1, two))


def _score_bound(q_gain, k_gain):
    gq = jnp.max(jnp.abs(q_gain.astype(F32)))
    gk = jnp.max(jnp.abs(k_gain.astype(F32)))
    return BOUND_MARGIN * HEAD_DIM ** 0.5 * LOG2_E * gq * gk


def _tiles(T):
    tm = min(512, T)
    tq = min(512, T)
    tk = min(1024, T)
    return tm, tq, tk


def _trunk(x, mem, p):
    T = x.shape[1]
    tm, tq, tk = _tiles(T)
    cos, sup, sdn = _rope_tables(T)
    kt, va = _mem_kv(mem, p["mem_norm"], p["w_mem_kv"])
    q, k, v, u = _in_proj(x, p["attn_norm"], p["w_a"], p["gq"], p["gk"], p["bd"], cos, sup, sdn, 2 * tm)
    attn = lax.cond(p["score_bound"] <= SAFE_EXP2_RANGE,
                    functools.partial(_flash, tq=tq, tk=tk, bounded=True),
                    functools.partial(_flash, tq=tq, tk=tk, bounded=False), q, k, v)
    h = _merge(x, attn, u, kt, va, p["attn_norm"], p["w_b"], p["w_attn_o"], p["w_conv_o"],
               p["w_mem_o"], p["w_out"], p["conv_w"], p["conv_b"], p["conv_ln_g"], p["conv_ln_b"],
               tm, 4)
    return _mlp(h, p["mlp_norm"], p["w_up"], p["w_down"], p["final_norm"], 2 * tm)


def kernel(x_prompt, x_sample, mem_prompt, mem_sample, attn_norm, w_in, q_norm, k_norm, w_attn_o,
           conv_w, conv_b, conv_ln_g, conv_ln_b, w_conv_o, mem_norm, w_mem_kv, w_mem_o, w_out,
           mlp_norm, w_up, w_down, final_norm):
    assert w_in.shape == (1, D_MODEL, IN_COLS)
    row = lambda a: a.reshape(1, -1).astype(F32)
    two = LANES // HEAD_DIM
    head_id = jnp.arange(MXU_WIDTH) // HEAD_DIM
    p = {
        "attn_norm": row(attn_norm[0]),
        "w_a": w_in[0, :, :OFF_XQ].astype(BF16),
        "w_b": w_in[0, :, OFF_XQ:].astype(BF16),
        "gq": row(jnp.tile(q_norm[0], two)),
        "gk": row(jnp.tile(k_norm[0], two)),
        "bd": (head_id[:, None] == head_id[None, :]).astype(BF16),
        "w_attn_o": w_attn_o[0].astype(BF16),
        "conv_w": conv_w[0].astype(F32),
        "conv_b": row(conv_b[0]),
        "conv_ln_g": row(conv_ln_g[0]),
        "conv_ln_b": row(conv_ln_b[0]),
        "w_conv_o": w_conv_o[0].astype(BF16),
        "mem_norm": row(mem_norm[0]),
        "w_mem_kv": w_mem_kv[0].astype(BF16),
        "w_mem_o": w_mem_o[0].astype(BF16),
        "w_out": w_out[0].astype(BF16),
        "mlp_norm": row(mlp_norm[0]),
        "w_up": w_up[0].astype(BF16),
        "w_down": w_down[0].astype(BF16),
        "final_norm": row(final_norm),
        "score_bound": _score_bound(q_norm[0], k_norm[0]),
    }
    return (_trunk(x_prompt, mem_prompt, p), _trunk(x_sample, mem_sample, p))
```

```python
import functools

import jax
import jax.numpy as jnp
from jax import lax
from jax.experimental import pallas as pl
from jax.experimental.pallas import tpu as pltpu

F32 = jnp.float32
BF16 = jnp.bfloat16

D_MODEL = 1024
GRID_W = 64
N_HEADS = 8
N_KV_HEADS = 2
HEAD_DIM = 64
GQA_GROUP = N_HEADS // N_KV_HEADS
ATTN_W = N_HEADS * HEAD_DIM
KV_W = N_KV_HEADS * HEAD_DIM
GROUP_W = GQA_GROUP * HEAD_DIM
ROPE_THETA = 10000.0
ROPE_PAIRS = HEAD_DIM // 4
CONV_CH = 512
CONV_WIDTH = 31
CONV_PAD = CONV_WIDTH // 2
N_MEM = 256
MEM_HEADS = 4
MEM_HEAD_DIM = 128
MEM_W = MEM_HEADS * MEM_HEAD_DIM
N_BRANCH = 3
D_FF = 4 * D_MODEL
RMS_EPS = 1e-6
LN_EPS = 1e-5
OFF_Q = 0
OFF_K = OFF_Q + ATTN_W
OFF_V = OFF_K + KV_W
OFF_GLU = OFF_V + KV_W
OFF_XQ = OFF_GLU + 2 * CONV_CH
OFF_GATE = OFF_XQ + MEM_W
IN_COLS = OFF_GATE + N_BRANCH * D_MODEL

LANES = 128
SUBLANES = 8
MXU_WIDTH = 256
PV_ROWS = HEAD_DIM + 8
CONV_HALO = 16
NEG_BIG = -1e30
LOG2_E = 1.4426950408889634
SAFE_EXP2_RANGE = 40.0
BOUND_MARGIN = 1.05
VMEM_LIMIT_BYTES = 56 * 1024 * 1024


def _params(n_grid_dims, flags=None):
    return pltpu.CompilerParams(
        dimension_semantics=("arbitrary",) * n_grid_dims,
        vmem_limit_bytes=VMEM_LIMIT_BYTES,
        flags=flags)


def _rms_norm(x, gain):
    ms = jnp.mean(x * x, axis=-1, keepdims=True)
    return x * lax.rsqrt(ms + RMS_EPS) * gain


def _mem_kv_kernel(mem_ref, g_ref, w_ref, kt_ref, va_ref):
    mn = _rms_norm(mem_ref[0], g_ref[...]).astype(BF16)
    kv = jnp.dot(mn, w_ref[...], preferred_element_type=F32)
    kt_ref[0] = kv[:, :MEM_W].T.astype(BF16)
    ones = jnp.ones((N_MEM, MEM_HEAD_DIM), BF16)
    for h in range(MEM_HEADS):
        lo = MEM_W + h * MEM_HEAD_DIM
        va_ref[0, h, :, :MEM_HEAD_DIM] = kv[:, lo:lo + MEM_HEAD_DIM].astype(BF16)
        va_ref[0, h, :, MEM_HEAD_DIM:] = ones


def _mem_kv(mem, g, w):
    B = mem.shape[0]
    return pl.pallas_call(
        _mem_kv_kernel,
        grid=(B,),
        in_specs=[
            pl.BlockSpec((1, N_MEM, D_MODEL), lambda b: (b, 0, 0)),
            pl.BlockSpec((1, D_MODEL), lambda b: (0, 0)),
            pl.BlockSpec((D_MODEL, 2 * MEM_W), lambda b: (0, 0)),
        ],
        out_specs=[
            pl.BlockSpec((1, MEM_W, N_MEM), lambda b: (b, 0, 0)),
            pl.BlockSpec((1, MEM_HEADS, N_MEM, 2 * MEM_HEAD_DIM), lambda b: (b, 0, 0, 0)),
        ],
        out_shape=[
            jax.ShapeDtypeStruct((B, MEM_W, N_MEM), BF16),
            jax.ShapeDtypeStruct((B, MEM_HEADS, N_MEM, 2 * MEM_HEAD_DIM), BF16),
        ],
        compiler_params=_params(1),
        name="mem_kv",
    )(mem, g, w)


def _head_norm_rope(z, gain, bd, cos, sin_up, sin_dn, scale):
    width = z.shape[1]
    rep = width // LANES
    wide = lambda t: t if rep == 1 else jnp.concatenate([t] * rep, axis=1)
    ssq = jnp.dot((z * z).astype(BF16), bd[:width, :width], preferred_element_type=F32)
    y = z * lax.rsqrt(ssq * (1.0 / HEAD_DIM) + RMS_EPS) * wide(gain)
    up = pltpu.roll(y, width - ROPE_PAIRS, 1)
    dn = pltpu.roll(y, ROPE_PAIRS, 1)
    return (y * wide(cos) + up * wide(sin_up) + dn * wide(sin_dn)) * scale


def _in_proj_kernel(x_ref, g_ref, w_ref, gq_ref, gk_ref, bd_ref, cos_ref, sup_ref, sdn_ref,
                    q_ref, k_ref, v_ref, u_ref):
    xn = _rms_norm(x_ref[0], g_ref[...]).astype(BF16)
    proj = lambda lo, hi: jnp.dot(xn, w_ref[:, lo:hi], preferred_element_type=F32)
    bd = bd_ref[...]
    cos, sup, sdn = cos_ref[...], sup_ref[...], sdn_ref[...]
    q_scale = HEAD_DIM ** -0.5 * LOG2_E
    zq = proj(OFF_Q, OFF_K)
    zkv = proj(OFF_K, OFF_GLU)
    for c in range(ATTN_W // MXU_WIDTH):
        cols = slice(c * MXU_WIDTH, (c + 1) * MXU_WIDTH)
        q_ref[0, :, cols] = _head_norm_rope(zq[:, cols], gq_ref[...], bd, cos, sup, sdn,
                                            q_scale).astype(BF16)
    za = proj(OFF_GLU, OFF_GLU + CONV_CH)
    kn = _head_norm_rope(zkv[:, :KV_W], gk_ref[...], bd, cos, sup, sdn, 1.0).astype(BF16)
    for h in range(N_KV_HEADS):
        k_ref[0, h] = kn[:, h * HEAD_DIM:(h + 1) * HEAD_DIM]
    v_ref[0] = zkv[:, KV_W:].astype(BF16)
    zb = proj(OFF_GLU + CONV_CH, OFF_XQ)
    u_ref[0] = (za * jax.nn.sigmoid(zb)).astype(BF16)


def _in_proj(x, g, w, gq, gk, bd, cos, sup, sdn, tm):
    B, T, _ = x.shape
    const = lambda b, i: (0, 0)
    tab = pl.BlockSpec((tm, LANES), lambda b, i: (i, 0))
    return pl.pallas_call(
        _in_proj_kernel,
        grid=(B, T // tm),
        in_specs=[
            pl.BlockSpec((1, tm, D_MODEL), lambda b, i: (b, i, 0)),
            pl.BlockSpec((1, D_MODEL), const),
            pl.BlockSpec((D_MODEL, OFF_XQ), const),
            pl.BlockSpec((1, LANES), const),
            pl.BlockSpec((1, LANES), const),
            pl.BlockSpec((MXU_WIDTH, MXU_WIDTH), const),
            tab, tab, tab,
        ],
        out_specs=[
            pl.BlockSpec((1, tm, ATTN_W), lambda b, i: (b, i, 0)),
            pl.BlockSpec((1, N_KV_HEADS, tm, HEAD_DIM), lambda b, i: (b, 0, i, 0)),
            pl.BlockSpec((1, tm, KV_W), lambda b, i: (b, i, 0)),
            pl.BlockSpec((1, tm, CONV_CH), lambda b, i: (b, i, 0)),
        ],
        out_shape=[
            jax.ShapeDtypeStruct((B, T, ATTN_W), BF16),
            jax.ShapeDtypeStruct((B, N_KV_HEADS, T, HEAD_DIM), BF16),
            jax.ShapeDtypeStruct((B, T, KV_W), BF16),
            jax.ShapeDtypeStruct((B, T, CONV_CH), BF16),
        ],
        compiler_params=_params(2),
        name="in_proj",
    )(x, g, w, gq, gk, bd, cos, sup, sdn)


def _flash_kernel(q_ref, qn_ref, k_ref, v_ref, o_ref, vt_ref, qp_ref, m_ref, acc_ref,
                  s_a, s_b, mx_a, mx_b, *, tk, tq, tc, n_kv, bounded):
    g = pl.program_id(1)
    j = pl.program_id(2)
    is_g0 = g == 0
    n_ch = tk // tc

    def make_qp(src_ref, slot):
        qt = src_ref[0].astype(F32).T
        for i in range(GQA_GROUP):
            qp_ref[slot, i] = qt[i * HEAD_DIM:(i + 1) * HEAD_DIM].astype(BF16)

    def qk_head(n, slot, i, s_out, mx_out):
        base = pl.multiple_of(n * tk, tk)
        mi = None
        for c in range(n_ch):
            kc = k_ref[0, 0, pl.ds(base + c * tc, tc), :]
            s = jnp.dot(kc, qp_ref[slot, i], preferred_element_type=F32)
            s_out[i, c * tc:(c + 1) * tc, :] = s
            if not bounded:
                mc = jnp.max(s, axis=0, keepdims=True)
                mi = mc if mi is None else jnp.maximum(mi, mc)
        if not bounded:
            mx_out[i] = mi

    def fused_head(n_next, slot, vtb, i, s_in, mx_in, s_out, mx_out):
        base = pl.multiple_of(n_next * tk, tk)
        if not bounded:
            m_old = m_ref[i]
            m_new = jnp.maximum(m_old, mx_in[i])
        mi = None
        pv = None
        for c in range(n_ch):
            rows = slice(c * tc, (c + 1) * tc)
            kc = k_ref[0, 0, pl.ds(base + c * tc, tc), :]
            s = jnp.dot(kc, qp_ref[slot, i], preferred_element_type=F32)
            s_out[i, rows, :] = s
            if not bounded:
                mc = jnp.max(s, axis=0, keepdims=True)
                mi = mc if mi is None else jnp.maximum(mi, mc)
            e = s_in[i, rows, :] if bounded else s_in[i, rows, :] - m_new
            p = jnp.exp2(e)
            d = jnp.dot(vtb[:, rows], p, preferred_element_type=F32)
            pv = d if pv is None else pv + d
        if bounded:
            acc_ref[i] = acc_ref[i] + pv
        else:
            mx_out[i] = mi
            acc_ref[i] = acc_ref[i] * jnp.exp2(m_old - m_new) + pv
            m_ref[i] = m_new

    buf_a = (s_a, mx_a)
    buf_b = (s_b, mx_b)

    @pl.when(j == 0)
    def _():
        ones = jnp.ones((PV_ROWS - HEAD_DIM, tk), F32)

        def build(c, carry):
            vc = v_ref[0, pl.ds(pl.multiple_of(c * tk, tk), tk), :].astype(F32)
            vct = vc.T
            top = jnp.where(is_g0, vct[:HEAD_DIM], vct[HEAD_DIM:])
            vt_ref[c, :HEAD_DIM, :] = top
            vt_ref[c, HEAD_DIM:, :] = ones
            return carry

        lax.fori_loop(0, n_kv, build, 0)
        make_qp(q_ref, 0)
        for i in range(GQA_GROUP):
            qk_head(0, 0, i, *buf_a)

    @pl.when(j > 0)
    def _():
        qp_ref[0] = qp_ref[1]

    make_qp(qn_ref, 1)
    if not bounded:
        m_ref[...] = jnp.full(m_ref.shape, NEG_BIG, F32)
    acc_ref[...] = jnp.zeros(acc_ref.shape, F32)

    def step(n, cur, nxt, n_next, slot):
        vtb = vt_ref[n]
        for i in range(GQA_GROUP):
            fused_head(n_next, slot, vtb, i, *cur, *nxt)

    def pair(t, wrap):
        n = 2 * t
        step(n, buf_a, buf_b, n + 1, 0)
        if wrap:
            step(n + 1, buf_b, buf_a, 0, 1)
        else:
            step(n + 1, buf_b, buf_a, n + 2, 0)

    def loop_body(t, carry):
        pair(t, False)
        return carry

    lax.fori_loop(0, n_kv // 2 - 1, loop_body, 0)
    pair(n_kv // 2 - 1, True)

    outs = []
    for i in range(GQA_GROUP):
        a = acc_ref[i]
        outs.append(a[:HEAD_DIM] / a[HEAD_DIM:HEAD_DIM + 1])
    o_ref[0] = jnp.concatenate(outs, axis=0).T.astype(BF16)


def _flash(q, k, v, tq, tk, bounded):
    B, T, _ = q.shape
    n_kv = T // tk
    n_q = T // tq
    assert n_kv % 2 == 0
    tc = min(256, tk)
    kern = functools.partial(_flash_kernel, tk=tk, tq=tq, tc=tc, n_kv=n_kv, bounded=bounded)
    s_buf = pltpu.VMEM((GQA_GROUP, tk, tq), F32)
    row_buf = pltpu.VMEM((GQA_GROUP, 1, tq), F32)
    return pl.pallas_call(
        kern,
        grid=(B, N_KV_HEADS, n_q),
        in_specs=[
            pl.BlockSpec((1, tq, GROUP_W), lambda b, g, j: (b, j, g)),
            pl.BlockSpec((1, tq, GROUP_W), lambda b, g, j: (b, jnp.minimum(j + 1, n_q - 1), g)),
            pl.BlockSpec((1, 1, T, HEAD_DIM), lambda b, g, j: (b, g, 0, 0)),
            pl.BlockSpec((1, T, KV_W), lambda b, g, j: (b, 0, 0)),
        ],
        out_specs=pl.BlockSpec((1, tq, GROUP_W), lambda b, g, j: (b, j, g)),
        out_shape=jax.ShapeDtypeStruct((B, T, ATTN_W), BF16),
        scratch_shapes=[
            pltpu.VMEM((n_kv, PV_ROWS, tk), F32),
            pltpu.VMEM((2, GQA_GROUP, HEAD_DIM, tq), BF16),
            row_buf,
            pltpu.VMEM((GQA_GROUP, PV_ROWS, tq), F32),
            s_buf, s_buf, row_buf, row_buf,
        ],
        compiler_params=_params(3),
        name="flash",
    )(q, q, k, v)


def _conv_fill(i, last, prev_ref, cur_ref, next_ref, sel_ref, tail_ref, a_ref, tm):
    n_hi = 2 * CONV_HALO * SUBLANES
    halo = jnp.zeros((CONV_HALO, CONV_CH), BF16)
    head = jnp.concatenate([jnp.where(i > 0, prev_ref[0], halo), cur_ref[0, :tm - CONV_HALO, :]],
                           axis=0)
    tail = jnp.concatenate([cur_ref[0, tm - CONV_HALO:, :], jnp.where(i < last, next_ref[0], halo)],
                           axis=0)
    a_lo = jnp.dot(sel_ref[...], head, preferred_element_type=F32)
    a_ref[:tm, :] = a_lo
    up = pltpu.roll(a_lo[:n_hi], n_hi - 1, 0)
    a_tail = jnp.dot(tail_ref[...], tail, preferred_element_type=F32)
    sub = lax.broadcasted_iota(jnp.int32, (n_hi, CONV_CH), 0) % SUBLANES
    a_ref[tm:, :] = jnp.where(sub == SUBLANES - 1, a_tail, up)


def _conv_taps(a_ref, pre_ref, w_ref, b_ref, p0, slabs):
    first = CONV_HALO - CONV_PAD
    n_lt = CONV_CH // LANES
    acc = [[b_ref[:, j * LANES:(j + 1) * LANES] + jnp.zeros((SUBLANES, LANES), F32)
            for j in range(n_lt)] for _ in range(slabs)]
    for tap in range(CONV_WIDTH):
        for j in range(n_lt):
            lanes = slice(j * LANES, (j + 1) * LANES)
            wt = w_ref[tap * SUBLANES:(tap + 1) * SUBLANES, lanes]
            for sl in range(slabs):
                start = (p0 + sl + tap + first) * SUBLANES
                acc[sl][j] = acc[sl][j] + a_ref[start:start + SUBLANES, lanes] * wt
    for sl in range(slabs):
        start = (p0 + sl) * SUBLANES
        pre_ref[start:start + SUBLANES, :] = jnp.concatenate(acc[sl], axis=1)


def _conv_norm(pre_ref, unsel_ref, lg_ref, lb_ref):
    acc = pre_ref[...]
    mu = jnp.mean(acc, axis=-1, keepdims=True)
    xc = acc - mu
    var = jnp.mean(xc * xc, axis=-1, keepdims=True)
    y = xc * lax.rsqrt(var + LN_EPS) * lg_ref[...] + lb_ref[...]
    y = (y * jax.nn.sigmoid(y)).astype(BF16)
    return jnp.dot(unsel_ref[...], y, preferred_element_type=F32).astype(BF16)


def _conv_perms(tm):
    seg = tm // SUBLANES
    r = jnp.arange(tm)
    src = (r % SUBLANES) * seg + r // SUBLANES
    sel = (src[:, None] == jnp.arange(tm)[None, :]).astype(BF16)
    n_hi = 2 * CONV_HALO * SUBLANES
    rh = jnp.arange(n_hi)
    tail = ((rh % SUBLANES == SUBLANES - 1)[:, None]
            & (rh[:, None] // SUBLANES == jnp.arange(2 * CONV_HALO)[None, :])).astype(BF16)
    return sel, tail, sel.T


def _merge_kernel(x_ref, attn_ref, up_ref, uc_ref, un_ref, kt_ref, va_ref, g_ref, wb_ref, wa_ref,
                  wc_ref, wm_ref, wo_ref, sel_ref, tail_ref, unsel_ref, cw_ref, cb_ref, lg_ref, lb_ref,
                  h_ref, a_ref, pre_ref, *, tm, slabs):
    i = pl.program_id(1)
    last = pl.num_programs(1) - 1
    x = x_ref[0]
    xn = _rms_norm(x, g_ref[...]).astype(BF16)
    xq = jnp.dot(xn, wb_ref[:, :MEM_W], preferred_element_type=F32).astype(BF16)
    mem_scale = MEM_HEAD_DIM ** -0.5
    scores = []
    for h in range(MEM_HEADS):
        lo = h * MEM_HEAD_DIM
        scores.append(jnp.dot(xq[:, lo:lo + MEM_HEAD_DIM], kt_ref[0, lo:lo + MEM_HEAD_DIM, :],
                              preferred_element_type=F32) * mem_scale)
    _conv_fill(i, last, up_ref, uc_ref, un_ref, sel_ref, tail_ref, a_ref, tm)
    attn_out = jnp.dot(attn_ref[0], wa_ref[...], preferred_element_type=F32)
    zg = jnp.dot(xn, wb_ref[:, MEM_W:], preferred_element_type=F32)
    heads = []
    for h in range(MEM_HEADS):
        s = scores[h]
        p = jnp.exp(s - jnp.max(s, axis=-1, keepdims=True)).astype(BF16)
        oa = jnp.dot(p, va_ref[0, h], preferred_element_type=F32)
        heads.append(oa[:, :MEM_HEAD_DIM] / oa[:, MEM_HEAD_DIM:])
    mo = jnp.concatenate(heads, axis=-1).astype(BF16)
    mem_out = jnp.dot(mo, wm_ref[...], preferred_element_type=F32)
    for p0 in range(0, tm // SUBLANES, slabs):
        _conv_taps(a_ref, pre_ref, cw_ref, cb_ref, p0, slabs)
    c = _conv_norm(pre_ref, unsel_ref, lg_ref, lb_ref)
    conv_out = jnp.dot(c, wc_ref[...], preferred_element_type=F32)
    g0 = jax.nn.sigmoid(zg[:, :D_MODEL])
    g1 = jax.nn.sigmoid(zg[:, D_MODEL:2 * D_MODEL])
    g2 = jax.nn.sigmoid(zg[:, 2 * D_MODEL:])
    merged = (g0 * attn_out + g1 * conv_out + g2 * mem_out).astype(BF16)
    h_ref[0] = x + jnp.dot(merged, wo_ref[...], preferred_element_type=F32)


def _merge(x, attn, u, kt, va, g, wb, wa, wc, wm, wo, cw, cb, lg, lb, tm, slabs):
    B, T, _ = x.shape
    hb = tm // CONV_HALO
    n_halo_blocks = T // CONV_HALO
    seg = tm // SUBLANES
    assert seg % slabs == 0 and seg >= 2 * CONV_HALO
    sel, tail, unsel = _conv_perms(tm)
    const = lambda b, i: (0, 0)
    tok = lambda w: pl.BlockSpec((1, tm, w), lambda b, i: (b, i, 0))
    row = lambda w: pl.BlockSpec((1, w), const)
    kern = functools.partial(_merge_kernel, tm=tm, slabs=slabs)
    return pl.pallas_call(
        kern,
        grid=(B, T // tm),
        in_specs=[
            tok(D_MODEL), tok(ATTN_W),
            pl.BlockSpec((1, CONV_HALO, CONV_CH),
                         lambda b, i: (b, jnp.maximum(i * hb - 1, 0), 0)),
            tok(CONV_CH),
            pl.BlockSpec((1, CONV_HALO, CONV_CH),
                         lambda b, i: (b, jnp.minimum((i + 1) * hb, n_halo_blocks - 1), 0)),
            pl.BlockSpec((1, MEM_W, N_MEM), lambda b, i: (b, 0, 0)),
            pl.BlockSpec((1, MEM_HEADS, N_MEM, 2 * MEM_HEAD_DIM), lambda b, i: (b, 0, 0, 0)),
            row(D_MODEL),
            pl.BlockSpec((D_MODEL, MEM_W + N_BRANCH * D_MODEL), const),
            pl.BlockSpec((ATTN_W, D_MODEL), const),
            pl.BlockSpec((CONV_CH, D_MODEL), const),
            pl.BlockSpec((MEM_W, D_MODEL), const),
            pl.BlockSpec((D_MODEL, D_MODEL), const),
            pl.BlockSpec(sel.shape, const),
            pl.BlockSpec(tail.shape, const),
            pl.BlockSpec(unsel.shape, const),
            pl.BlockSpec((CONV_WIDTH * SUBLANES, CONV_CH), const),
            row(CONV_CH), row(CONV_CH), row(CONV_CH),
        ],
        out_specs=tok(D_MODEL),
        out_shape=jax.ShapeDtypeStruct((B, T, D_MODEL), F32),
        scratch_shapes=[pltpu.VMEM(((seg + 2 * CONV_HALO) * SUBLANES, CONV_CH), F32),
                        pltpu.VMEM((tm, CONV_CH), F32)],
        compiler_params=_params(2),
        name="merge",
    )(x, attn, u, u, u, kt, va, g, wb, wa, wc, wm, wo, sel, tail, unsel,
      jnp.repeat(cw, SUBLANES, axis=0), cb, lg, lb)


def _mlp_kernel(h_ref, g_ref, wu_ref, wd_ref, gf_ref, y_ref):
    h = h_ref[0]
    hn = _rms_norm(h, g_ref[...]).astype(BF16)
    a = jnp.dot(hn, wu_ref[...], preferred_element_type=F32)
    a = jnp.square(jnp.maximum(a, 0.0)).astype(BF16)
    y = h + jnp.dot(a, wd_ref[...], preferred_element_type=F32)
    y_ref[0] = _rms_norm(y, gf_ref[...])


def _mlp(h, g, wu, wd, gf, tm):
    B, T, _ = h.shape
    const = lambda b, i: (0, 0)
    return pl.pallas_call(
        _mlp_kernel,
        grid=(B, T // tm),
        in_specs=[
            pl.BlockSpec((1, tm, D_MODEL), lambda b, i: (b, i, 0)),
            pl.BlockSpec((1, D_MODEL), const),
            pl.BlockSpec((D_MODEL, D_FF), const),
            pl.BlockSpec((D_FF, D_MODEL), const),
            pl.BlockSpec((1, D_MODEL), const),
        ],
        out_specs=pl.BlockSpec((1, tm, D_MODEL), lambda b, i: (b, i, 0)),
        out_shape=jax.ShapeDtypeStruct((B, T, D_MODEL), F32),
        compiler_params=_params(2),
        name="mlp",
    )(h, g, wu, wd, gf)


def _rope_tables(T):
    t = jnp.arange(T)
    inv_freq = ROPE_THETA ** (-jnp.arange(ROPE_PAIRS, dtype=F32) / ROPE_PAIRS)
    ang_r = (t // GRID_W).astype(F32)[:, None] * inv_freq[None, :]
    ang_c = (t % GRID_W).astype(F32)[:, None] * inv_freq[None, :]
    zero = jnp.zeros_like(ang_r)
    cos_h = jnp.concatenate([jnp.cos(ang_r), jnp.cos(ang_r), jnp.cos(ang_c), jnp.cos(ang_c)], -1)
    sup_h = jnp.concatenate([-jnp.sin(ang_r), zero, -jnp.sin(ang_c), zero], -1)
    sdn_h = jnp.concatenate([zero, jnp.sin(ang_r), zero, jnp.sin(ang_c)], -1)
    two = LANES // HEAD_DIM
    return jnp.tile(cos_h, (1, two)), jnp.tile(sup_h, (1, two)), jnp.tile(sdn_h, (1, two))


def _score_bound(q_gain, k_gain):
    gq = jnp.max(jnp.abs(q_gain.astype(F32)))
    gk = jnp.max(jnp.abs(k_gain.astype(F32)))
    return BOUND_MARGIN * HEAD_DIM ** 0.5 * LOG2_E * gq * gk


def _tiles(T):
    tm = min(512, T)
    tq = min(512, T)
    tk = min(1024, T)
    return tm, tq, tk


def _trunk(x, mem, p):
    T = x.shape[1]
    tm, tq, tk = _tiles(T)
    cos, sup, sdn = _rope_tables(T)
    kt, va = _mem_kv(mem, p["mem_norm"], p["w_mem_kv"])
    q, k, v, u = _in_proj(x, p["attn_norm"], p["w_a"], p["gq"], p["gk"], p["bd"], cos, sup, sdn, 2 * tm)
    attn = lax.cond(p["score_bound"] <= SAFE_EXP2_RANGE,
                    functools.partial(_flash, tq=tq, tk=tk, bounded=True),
                    functools.partial(_flash, tq=tq, tk=tk, bounded=False), q, k, v)
    h = _merge(x, attn, u, kt, va, p["attn_norm"], p["w_b"], p["w_attn_o"], p["w_conv_o"],
               p["w_mem_o"], p["w_out"], p["conv_w"], p["conv_b"], p["conv_ln_g"], p["conv_ln_b"],
               tm, 4)
    return _mlp(h, p["mlp_norm"], p["w_up"], p["w_down"], p["final_norm"], 2 * tm)


def kernel(x_prompt, x_sample, mem_prompt, mem_sample, attn_norm, w_in, q_norm, k_norm, w_attn_o,
           conv_w, conv_b, conv_ln_g, conv_ln_b, w_conv_o, mem_norm, w_mem_kv, w_mem_o, w_out,
           mlp_norm, w_up, w_down, final_norm):
    assert w_in.shape == (1, D_MODEL, IN_COLS)
    row = lambda a: a.reshape(1, -1).astype(F32)
    two = LANES // HEAD_DIM
    head_id = jnp.arange(MXU_WIDTH) // HEAD_DIM
    p = {
        "attn_norm": row(attn_norm[0]),
        "w_a": w_in[0, :, :OFF_XQ].astype(BF16),
        "w_b": w_in[0, :, OFF_XQ:].astype(BF16),
        "gq": row(jnp.tile(q_norm[0], two)),
        "gk": row(jnp.tile(k_norm[0], two)),
        "bd": (head_id[:, None] == head_id[None, :]).astype(BF16),
        "w_attn_o": w_attn_o[0].astype(BF16),
        "conv_w": conv_w[0].astype(F32),
        "conv_b": row(conv_b[0]),
        "conv_ln_g": row(conv_ln_g[0]),
        "conv_ln_b": row(conv_ln_b[0]),
        "w_conv_o": w_conv_o[0].astype(BF16),
        "mem_norm": row(mem_norm[0]),
        "w_mem_kv": w_mem_kv[0].astype(BF16),
        "w_mem_o": w_mem_o[0].astype(BF16),
        "w_out": w_out[0].astype(BF16),
        "mlp_norm": row(mlp_norm[0]),
        "w_up": w_up[0].astype(BF16),
        "w_down": w_down[0].astype(BF16),
        "final_norm": row(final_norm),
        "score_bound": _score_bound(q_norm[0], k_norm[0]),
    }
    return (_trunk(x_prompt, mem_prompt, p), _trunk(x_sample, mem_sample, p))
```
